```python
import math
import jax, jax.numpy as jnp
from jax import lax
import numpy as np

D_MODEL = 1024
BATCH = 8
SEQ = 4096
DEPTH = 1

N_META = 16
BLOCK = 128
PREFIX = BLOCK
N_PAD = PREFIX - N_META

GLA_HEADS = 4
GLA_DK = 64
GLA_DV = 128
GLA_GATE_RANK = 16
GLA_TAU = 16.0
GLA_CHUNK = 64
GLA_QK = GLA_HEADS * GLA_DK
GLA_V = GLA_HEADS * GLA_DV

FOX_HEADS = 8
FOX_DH = 64
FOX_W = FOX_HEADS * FOX_DH

MIX_W = GLA_V + FOX_W
IN_SPLITS = (GLA_QK, GLA_QK, GLA_V, GLA_V, GLA_GATE_RANK, FOX_W, FOX_W, FOX_W, FOX_HEADS)
IN_W = 2 * GLA_QK + 2 * GLA_V + GLA_GATE_RANK + 3 * FOX_W + FOX_HEADS

PEER_HEADS = 8
PEER_NKEYS = 128
PEER_EXPERTS = PEER_NKEYS * PEER_NKEYS
PEER_DKEY = 256
PEER_TOPK = 16
PEER_TOK_BLOCK = 128

DN_ALPHA = (2.0 * DEPTH) ** 0.25
DN_BETA = (8.0 * DEPTH) ** -0.25
LN_EPS = 1e-5
NEG = -1e30

kernel_name = "hymba_gla_fox_peer_deepnorm"


def layer_norm(x, g, b):
    xf = x.astype(jnp.float32)
    mu = jnp.mean(xf, -1, keepdims=True)
    var = jnp.mean(jnp.square(xf - mu), -1, keepdims=True)
    return ((xf - mu) * lax.rsqrt(var + LN_EPS) * g.astype(jnp.float32) + b.astype(jnp.float32)).astype(x.dtype)


def head_rmsnorm(o, g):
    r = o * lax.rsqrt(jnp.mean(jnp.square(o), -1, keepdims=True) + LN_EPS)
    return r * g.astype(jnp.float32).reshape(o.shape[-2:])


def gla_mix(q, k, v, glog, valid):
    B, L, H, dk = q.shape
    dv = v.shape[-1]
    C = GLA_CHUNK
    N = L // C
    scale = dk ** -0.5
    k = k * valid[None, :, None, None].astype(k.dtype)

    def chunked(t):
        return t.reshape(B, N, C, H, t.shape[-1]).transpose(0, 3, 1, 2, 4)

    q, k, v, glog = chunked(q), chunked(k), chunked(v), chunked(glog)
    bcum = jnp.cumsum(glog, axis=3)
    b_ref = bcum[:, :, :, C // 2 - 1:C // 2, :]
    q_in = q * jnp.exp(bcum - b_ref)
    k_in = k * jnp.exp(b_ref - bcum)
    causal = jnp.tril(jnp.ones((C, C), dtype=bool))
    a = jnp.einsum('bhncd,bhnsd->bhncs', q_in, k_in) * scale
    a = jnp.where(causal, a, 0.0)
    o_intra = jnp.einsum('bhncs,bhnsv->bhncv', a, v)

    b_last = bcum[:, :, :, -1:, :]
    d_state = jnp.einsum('bhncd,bhncv->bhndv', k * jnp.exp(b_last - bcum), v)
    decay = jnp.exp(b_last[:, :, :, 0, :])[..., None]

    def step(S, inp):
        dec, ds = inp
        return dec * S + ds, S

    S0 = jnp.zeros((B, H, dk, dv), jnp.float32)
    _, S_prev = lax.scan(step, S0, (jnp.moveaxis(decay, 2, 0), jnp.moveaxis(d_state, 2, 0)))
    S_prev = jnp.moveaxis(S_prev, 0, 2)
    o_inter = jnp.einsum('bhncd,bhndv->bhncv', q * jnp.exp(bcum), S_prev) * scale
    o = o_intra + o_inter
    return o.transpose(0, 2, 3, 1, 4).reshape(B, L, H, dv)


def fox_mix(q, k, v, logf, valid):
    B, L, H, d = q.shape
    scale = d ** -0.5
    q, k, v = (t.transpose(0, 2, 1, 3) for t in (q, k, v))
    c = jnp.cumsum(logf, axis=1).transpose(0, 2, 1)
    pos = jnp.arange(L)
    outs = []
    for i in range(L // BLOCK):
        s0, e = i * BLOCK, (i + 1) * BLOCK
        logits = jnp.einsum('bhqd,bhkd->bhqk', q[:, :, s0:e], k[:, :, :e]) * scale
        logits = logits + c[:, :, s0:e, None] - c[:, :, None, :e]
        mask = (pos[None, :e] <= pos[s0:e, None]) & valid[None, :e]
        p = jax.nn.softmax(jnp.where(mask, logits, NEG), axis=-1)
        outs.append(jnp.einsum('bhqk,bhkd->bhqd', p, v[:, :, :e]))
    o = jnp.concatenate(outs, axis=2)
    return o.transpose(0, 2, 1, 3)


def hybrid_mixer(h, w_in, w_gate_up, b_gate, b_forget, gla_norm_g, fox_norm_g, w_out, valid):
    B, L, _ = h.shape
    proj = (h @ w_in).astype(jnp.float32)
    bounds, acc = [], 0
    for w in IN_SPLITS[:-1]:
        acc += w
        bounds.append(acc)
    qa, ka, va, ra, ga, qb, kb, vb, fb = jnp.split(proj, bounds, axis=-1)

    glog = jax.nn.log_sigmoid(ga @ w_gate_up.astype(jnp.float32) + b_gate.astype(jnp.float32)) / GLA_TAU
    hA = lambda t, d: t.reshape(B, L, GLA_HEADS, d)
    oa = gla_mix(hA(qa, GLA_DK), hA(ka, GLA_DK), hA(va, GLA_DV), hA(glog, GLA_DK), valid)
    oa = head_rmsnorm(oa, gla_norm_g).reshape(B, L, GLA_V) * jax.nn.silu(ra)

    logf = jax.nn.log_sigmoid(fb + b_forget.astype(jnp.float32))
    hB = lambda t: t.reshape(B, L, FOX_HEADS, FOX_DH)
    ob = fox_mix(hB(qb), hB(kb), hB(vb), logf, valid)
    ob = head_rmsnorm(ob, fox_norm_g).reshape(B, L, FOX_W)

    o = jnp.concatenate([oa, ob], axis=-1).astype(h.dtype)
    return o @ w_out


def peer(h, w_q, sub_keys, u_tab, v_tab):
    B, L, D = h.shape
    T = B * L
    xt = h.reshape(T, D)
    q = (xt @ w_q).astype(jnp.float32).reshape(T, PEER_HEADS, 2, PEER_DKEY // 2)
    s = jnp.einsum('thcd,hcnd->thcn', q, sub_keys.astype(jnp.float32))
    top_s, top_i = lax.top_k(s, PEER_TOPK)
    cand_s = (top_s[:, :, 0, :, None] + top_s[:, :, 1, None, :]).reshape(T, PEER_HEADS, -1)
    cand_e = (top_i[:, :, 0, :, None] * PEER_NKEYS + top_i[:, :, 1, None, :]).reshape(T, PEER_HEADS, -1)
    best_s, best_j = lax.top_k(cand_s, PEER_TOPK)
    experts = jnp.take_along_axis(cand_e, best_j, axis=-1)
    gates = jax.nn.softmax(best_s, axis=-1)

    nb, tb, hk = T // PEER_TOK_BLOCK, PEER_TOK_BLOCK, PEER_HEADS * PEER_TOPK

    def block(args):
        xb, eb, gb = args
        act = jax.nn.gelu(jnp.einsum('td,tkd->tk', xb, u_tab[eb]), approximate=False)
        return jnp.einsum('tk,tkd->td', gb.astype(xb.dtype) * act, v_tab[eb])

    y = lax.map(block, (xt.reshape(nb, tb, D), experts.reshape(nb, tb, hk), gates.reshape(nb, tb, hk)))
    return y.reshape(B, L, D)


def setup_inputs(seed: int = 0) -> dict:
    key = jax.random.key(seed)
    ks = jax.random.split(key, 20)
    nrm = lambda k, shape: jax.random.normal(k, shape, jnp.float32)
    D = D_MODEL
    col_scale = jnp.concatenate([
        jnp.ones((2 * GLA_QK,)), jnp.full((GLA_V,), DN_BETA), jnp.ones((GLA_V + GLA_GATE_RANK + 2 * FOX_W,)),
        jnp.full((FOX_W,), DN_BETA), jnp.ones((FOX_HEADS,))]).astype(jnp.float32)
    return {
        "x": nrm(ks[0], (BATCH, SEQ, D)),
        "meta_tokens": nrm(ks[1], (N_META, D)),
        "emb_ln_g": 1.0 + 0.02 * nrm(ks[2], (D,)),
        "emb_ln_b": 0.02 * nrm(ks[3], (D,)),
        "w_in": nrm(ks[4], (DEPTH, D, IN_W)) * (D ** -0.5) * col_scale,
        "w_gate_up": nrm(ks[5], (DEPTH, GLA_GATE_RANK, GLA_QK)) * (GLA_GATE_RANK ** -0.5),
        "b_gate": 0.02 * nrm(ks[6], (DEPTH, GLA_QK)),
        "b_forget": jax.random.uniform(ks[7], (DEPTH, FOX_HEADS), jnp.float32, 0.0, 4.0),
        "gla_norm_g": 1.0 + 0.02 * nrm(ks[8], (DEPTH, GLA_V)),
        "fox_norm_g": 1.0 + 0.02 * nrm(ks[9], (DEPTH, FOX_W)),
        "w_out": nrm(ks[10], (DEPTH, MIX_W, D)) * (MIX_W ** -0.5) * DN_BETA,
        "ln1_g": 1.0 + 0.02 * nrm(ks[11], (DEPTH, D)),
        "ln1_b": 0.02 * nrm(ks[12], (DEPTH, D)),
        "peer_w_q": nrm(ks[13], (DEPTH, D, PEER_HEADS * PEER_DKEY)) * (D ** -0.5),
        "peer_sub_keys": nrm(ks[14], (DEPTH, PEER_HEADS, 2, PEER_NKEYS, PEER_DKEY // 2)) * ((PEER_DKEY // 2) ** -0.5),
        "peer_u": nrm(ks[15], (DEPTH, PEER_EXPERTS, D)) * (D ** -0.5),
        "peer_v": nrm(ks[16], (DEPTH, PEER_EXPERTS, D)) * DN_BETA * (PEER_HEADS ** -0.5),
        "ln2_g": 1.0 + 0.02 * nrm(ks[17], (DEPTH, D)),
        "ln2_b": 0.02 * nrm(ks[18], (DEPTH, D)),
    }


def reference(x, meta_tokens, emb_ln_g, emb_ln_b, w_in, w_gate_up, b_gate, b_forget, gla_norm_g,
              fox_norm_g, w_out, ln1_g, ln1_b, peer_w_q, peer_sub_keys, peer_u, peer_v, ln2_g, ln2_b):
    B = x.shape[0]
    pad = jnp.zeros((B, N_PAD, D_MODEL), x.dtype)
    meta = jnp.broadcast_to(meta_tokens[None].astype(x.dtype), (B, N_META, D_MODEL))
    h = jnp.concatenate([pad, meta, x], axis=1)
    h = layer_norm(h, emb_ln_g, emb_ln_b)
    L = h.shape[1]
    valid = jnp.arange(L) >= N_PAD
    for l in range(DEPTH):
        mix = hybrid_mixer(h, w_in[l], w_gate_up[l], b_gate[l], b_forget[l], gla_norm_g[l],
                           fox_norm_g[l], w_out[l], valid)
        h = layer_norm(DN_ALPHA * h + mix, ln1_g[l], ln1_b[l])
        ffn = peer(h, peer_w_q[l], peer_sub_keys[l], peer_u[l], peer_v[l])
        h = layer_norm(DN_ALPHA * h + ffn, ln2_g[l], ln2_b[l])
    return h[:, PREFIX:]
```

```python
import functools

import jax
import jax.numpy as jnp
from jax import lax
from jax.experimental import pallas as pl
from jax.experimental.pallas import tpu as pltpu

F32 = jnp.float32
BF16 = jnp.bfloat16

D_MODEL = 1024
N_META = 16
PREFIX = 128
N_PAD = PREFIX - N_META

GLA_HEADS = 4
GLA_DK = 64
GLA_DV = 128
GLA_RANK = 16
GLA_TAU = 16.0
GLA_CHUNK = 64
GLA_QK = GLA_HEADS * GLA_DK
GLA_V = GLA_HEADS * GLA_DV

FOX_HEADS = 8
FOX_DH = 64
FOX_W = FOX_HEADS * FOX_DH

PEER_HEADS = 8
PEER_NKEYS = 128
PEER_TOPK = 16
PEER_HALF = 128

DN_ALPHA = 2.0 ** 0.25
LN_EPS = 1e-5
NEG = -1e30

LANES = 128
VMEM_LIMIT = 48 * 1024 * 1024

_PAIRS = tuple((a, b) for a in range(PEER_TOPK) for b in range(PEER_TOPK)
               if (a + 1) * (b + 1) <= PEER_TOPK)
_N_CAND = 56


def _dot(a, b):
    return jnp.dot(a, b, preferred_element_type=F32)


def _dot_nt(a, b):
    return lax.dot_general(a, b, (((1,), (1,)), ((), ())), preferred_element_type=F32)


def _layer_norm(x, g, b):
    mu = jnp.mean(x, axis=-1, keepdims=True)
    xc = x - mu
    var = jnp.mean(xc * xc, axis=-1, keepdims=True)
    return xc * lax.rsqrt(var + LN_EPS) * g + b


def _split3(x):
    hi = x.astype(BF16)
    r1 = x - hi.astype(F32)
    mid = r1.astype(BF16)
    lo = (r1 - mid.astype(F32)).astype(BF16)
    return hi, mid, lo


def _in_proj_kernel(x_ref, g_ref, b_ref, wm_ref, wga_ref, wft_ref, wup_ref, bg_ref, bf_ref,
                    gq_ref, gk_ref, gv_ref, gr_ref, glog_ref, fq_ref, fk_ref, fv_ref, lf_ref):
    hn = _layer_norm(x_ref[0], g_ref[...], b_ref[...])
    hb = hn.astype(BF16)
    proj = _dot(hb, wm_ref[...])
    gq_ref[0] = proj[:, 0:256]
    gk_ref[0] = proj[:, 256:512]
    gv_ref[0] = proj[:, 512:1024].astype(BF16)
    gr_ref[0] = proj[:, 1024:1536]
    fq_ref[0] = (proj[:, 1536:2048] * (FOX_DH ** -0.5)).astype(BF16)
    fk_ref[0] = proj[:, 2048:2560].astype(BF16)
    fv_ref[0] = proj[:, 2560:3072].astype(BF16)
    ga = _dot(hb, wga_ref[...])
    z = _dot(ga.astype(BF16), wup_ref[...]) + bg_ref[...]
    glog_ref[0] = jax.nn.log_sigmoid(z) / GLA_TAU
    fbt = _dot_nt(wft_ref[...], hb)
    lf_ref[0] = jax.nn.log_sigmoid(fbt + bf_ref[...])


def _in_proj(x, ln_g, ln_b, wm, wga, wft, wup, bg, bf, *, tm):
    bsz, s, d = x.shape
    grid = (bsz, s // tm)
    tok = lambda w: pl.BlockSpec((1, tm, w), lambda b, i: (b, i, 0))
    full = lambda a: pl.BlockSpec(a.shape, lambda b, i: (0,) * a.ndim)
    out_shape = (
        jax.ShapeDtypeStruct((bsz, s, GLA_QK), F32),
        jax.ShapeDtypeStruct((bsz, s, GLA_QK), F32),
        jax.ShapeDtypeStruct((bsz, s, GLA_V), BF16),
        jax.ShapeDtypeStruct((bsz, s, GLA_V), F32),
        jax.ShapeDtypeStruct((bsz, s, GLA_QK), F32),
        jax.ShapeDtypeStruct((bsz, s, FOX_W), BF16),
        jax.ShapeDtypeStruct((bsz, s, FOX_W), BF16),
        jax.ShapeDtypeStruct((bsz, s, FOX_W), BF16),
        jax.ShapeDtypeStruct((bsz, 16, s), F32),
    )
    out_specs = (tok(GLA_QK), tok(GLA_QK), tok(GLA_V), tok(GLA_V), tok(GLA_QK),
                 tok(FOX_W), tok(FOX_W), tok(FOX_W),
                 pl.BlockSpec((1, 16, tm), lambda b, i: (b, 0, i)))
    return pl.pallas_call(
        _in_proj_kernel,
        out_shape=out_shape,
        grid=grid,
        in_specs=[tok(d), full(ln_g), full(ln_b), full(wm), full(wga), full(wft), full(wup),
                  full(bg), full(bf)],
        out_specs=out_specs,
        compiler_params=pltpu.CompilerParams(
            dimension_semantics=("parallel", "parallel"), vmem_limit_bytes=VMEM_LIMIT),
        name="in_proj",
    )(x, ln_g, ln_b, wm, wga, wft, wup, bg, bf)


def _cumsum_kernel(x_ref, o_ref, *, n_tiles, rel_end):
    ri = lax.broadcasted_iota(jnp.int32, (LANES, LANES), 0)
    ci = lax.broadcasted_iota(jnp.int32, (LANES, LANES), 1)
    upper = (ri <= ci).astype(BF16)
    carry = jnp.zeros((x_ref.shape[0], 1), F32)
    for j in range(n_tiles):
        sl = slice(j * LANES, (j + 1) * LANES)
        hi, mid, lo = _split3(x_ref[:, sl])
        cs = (_dot(hi, upper) + _dot(mid, upper)) + _dot(lo, upper) + carry
        o_ref[:, sl] = cs
        carry = cs[:, LANES - 1:LANES]
    if rel_end:
        o_ref[...] = o_ref[...] - carry


def _cumsum_lanes(x, *, rel_end):
    rows, n = x.shape
    return pl.pallas_call(
        functools.partial(_cumsum_kernel, n_tiles=n // LANES, rel_end=rel_end),
        out_shape=jax.ShapeDtypeStruct((rows, n), F32),
        name="cumsum",
    )(x)


def _gla_kernel(q_ref, k_ref, v_ref, r_ref, gl_ref, gn_ref, s0_ref, o_ref, sf_ref, s_scr,
                *, n_invalid):
    t = pl.program_id(1)

    @pl.when(t == 0)
    def _():
        s_scr[...] = s0_ref[...]

    tl = 2 * GLA_CHUNK
    ri = lax.broadcasted_iota(jnp.int32, (tl, tl), 0)
    ci = lax.broadcasted_iota(jnp.int32, (tl, tl), 1)
    same_chunk = (ri // GLA_CHUNK) == (ci // GLA_CHUNK)
    tri = ((ci <= ri) & same_chunk)
    tri_b = tri.astype(BF16)
    first = ri < GLA_CHUNK
    lane = lax.broadcasted_iota(jnp.int32, (1, tl), 1)
    row1 = lax.broadcasted_iota(jnp.int32, (tl, 1), 0)
    scale = GLA_DK ** -0.5
    valid = (row1 + t * tl) >= n_invalid

    for p in range(GLA_HEADS // 2):
        cs = slice(p * tl, (p + 1) * tl)
        g = gl_ref[0, :, cs]
        hi, mid, lo = _split3(g)
        bcum = (_dot(tri_b, hi) + _dot(tri_b, mid)) + _dot(tri_b, lo)
        half = GLA_CHUNK // 2
        bref = jnp.where(first, bcum[half - 1:half, :], bcum[GLA_CHUNK + half - 1:GLA_CHUNK + half, :])
        blast = jnp.where(first, bcum[GLA_CHUNK - 1:GLA_CHUNK, :], bcum[tl - 1:tl, :])
        q = q_ref[0, :, cs]
        k = jnp.where(valid, k_ref[0, :, cs], 0.0)
        q_in = q * jnp.exp(bcum - bref)
        k_in = (k * jnp.exp(bref - bcum)).astype(BF16)
        qe = q * jnp.exp(bcum)
        kdt = (k * jnp.exp(blast - bcum)).T
        bcum_t = bcum.T
        dec0 = jnp.exp(bcum_t[:, GLA_CHUNK - 1:GLA_CHUNK])
        dec1 = jnp.exp(bcum_t[:, tl - 1:tl])
        kdt0 = jnp.where(lane < GLA_CHUNK, kdt, 0.0).astype(BF16)
        kdt1 = jnp.where(lane >= GLA_CHUNK, kdt, 0.0).astype(BF16)

        v_h = [v_ref[0, :, (2 * p + hh) * GLA_DV:(2 * p + hh + 1) * GLA_DV] for hh in range(2)]
        s_in = s_scr[p]
        ds0 = jnp.where(first, _dot(kdt0, v_h[0]), _dot(kdt0, v_h[1]))
        ds1 = jnp.where(first, _dot(kdt1, v_h[0]), _dot(kdt1, v_h[1]))
        s_mid = dec0 * s_in + ds0
        s_out = dec1 * s_mid + ds1
        s_scr[p] = s_out
        sf_ref[0, p] = s_out
        s_in_b = s_in.astype(BF16)
        s_mid_b = s_mid.astype(BF16)

        for hh in range(2):
            h = 2 * p + hh
            mh = (lane // GLA_DK) == hh
            a = _dot_nt(jnp.where(mh, q_in, 0.0).astype(BF16), k_in) * scale
            a = jnp.where(tri, a, 0.0)
            o_intra = _dot(a.astype(BF16), v_h[hh])
            qe_m = jnp.where(mh, qe, 0.0).astype(BF16)
            o_inter = jnp.where(row1 < GLA_CHUNK, _dot(qe_m, s_in_b), _dot(qe_m, s_mid_b)) * scale
            o = o_intra + o_inter
            hs = slice(h * GLA_DV, (h + 1) * GLA_DV)
            rn = o * lax.rsqrt(jnp.mean(o * o, axis=-1, keepdims=True) + LN_EPS) * gn_ref[:, hs]
            ra = r_ref[0, :, hs]
            o_ref[0, :, hs] = (rn * (ra * jax.nn.sigmoid(ra))).astype(BF16)


def _gla(gq, gk, gv, gr, glog, gn, s0, *, n_invalid):
    bsz, s, _ = gq.shape
    tl = 2 * GLA_CHUNK
    grid = (bsz, s // tl)
    tok = lambda w: pl.BlockSpec((1, tl, w), lambda b, t: (b, t, 0))
    return pl.pallas_call(
        functools.partial(_gla_kernel, n_invalid=n_invalid),
        out_shape=(jax.ShapeDtypeStruct((bsz, s, GLA_V), BF16),
                   jax.ShapeDtypeStruct((bsz, 2, tl, GLA_DV), F32)),
        grid=grid,
        in_specs=[tok(GLA_QK), tok(GLA_QK), tok(GLA_V), tok(GLA_V), tok(GLA_QK),
                  pl.BlockSpec((1, GLA_V), lambda b, t: (0, 0)),
                  pl.BlockSpec((2, tl, GLA_DV), lambda b, t: (0, 0, 0))],
        out_specs=(tok(GLA_V), pl.BlockSpec((1, 2, tl, GLA_DV), lambda b, t: (b, 0, 0, 0))),
        scratch_shapes=[pltpu.VMEM((2, tl, GLA_DV), F32)],
        compiler_params=pltpu.CompilerParams(dimension_semantics=("parallel", "arbitrary")),
        name="gla",
    )(gq, gk, gv, gr, glog, gn, s0)


def _fox_kernel(q_ref, k_ref, v_ref, cx_ref, kp_ref, vp_ref, cm_ref, g_ref, o_ref,
                m_scr, l_scr, acc_scr, *, tq):
    qi = pl.program_id(2)
    q = q_ref[0]
    lane = lax.broadcasted_iota(jnp.int32, (1, LANES), 1)
    qstart = pl.multiple_of(qi * tq, tq)
    ri = lax.broadcasted_iota(jnp.int32, (tq, tq), 0)
    ci = lax.broadcasted_iota(jnp.int32, (tq, tq), 1)

    def update(s, vblk):
        m_prev = m_scr[...]
        m_new = jnp.maximum(m_prev, jnp.max(s, axis=-1, keepdims=True))
        alpha = jnp.exp(m_prev - m_new)
        p = jnp.exp(s - m_new)
        l_scr[...] = alpha * l_scr[...] + jnp.sum(p, axis=-1, keepdims=True)
        acc_scr[...] = alpha * acc_scr[...] + _dot(p.astype(BF16), vblk)
        m_scr[...] = m_new

    heads = []
    for hh in range(2):
        mh = (lane // FOX_DH) == hh
        qm = jnp.where(mh, q, jnp.zeros_like(q))
        cref = cx_ref[0, 0, hh:hh + 1, pl.ds(qstart, LANES)][:, 0:1]
        m_scr[...] = jnp.full(m_scr.shape, -jnp.inf, F32)
        l_scr[...] = jnp.zeros(l_scr.shape, F32)
        acc_scr[...] = jnp.zeros(acc_scr.shape, F32)

        s = _dot_nt(qm, kp_ref[0]) + (cref - cm_ref[0, hh:hh + 1, :])
        update(jnp.where(lane >= N_PAD, s, NEG), vp_ref[0])

        def body(j, carry):
            ks = pl.multiple_of(j * tq, tq)
            s = _dot_nt(qm, k_ref[0, pl.ds(ks, tq), :]) + (cref - cx_ref[0, 0, hh:hh + 1, pl.ds(ks, tq)])
            update(s, v_ref[0, pl.ds(ks, tq), :])
            return carry

        lax.fori_loop(0, qi, body, 0)

        s = _dot_nt(qm, k_ref[0, pl.ds(qstart, tq), :]) + (cref - cx_ref[0, 0, hh:hh + 1, pl.ds(qstart, tq)])
        update(jnp.where(ci <= ri, s, NEG), v_ref[0, pl.ds(qstart, tq), :])
        heads.append(acc_scr[...] / l_scr[...])

    first = lane < FOX_DH
    o = jnp.where(first, heads[0], heads[1])
    sq = o * o
    ms0 = jnp.sum(jnp.where(first, sq, 0.0), axis=-1, keepdims=True) / FOX_DH
    ms1 = jnp.sum(jnp.where(first, 0.0, sq), axis=-1, keepdims=True) / FOX_DH
    rs = jnp.where(first, lax.rsqrt(ms0 + LN_EPS), lax.rsqrt(ms1 + LN_EPS))
    o_ref[0] = (o * rs * g_ref[...]).astype(BF16)


def _fox(fq, fk, fv, cx, fk_pre, fv_pre, cmeta, gn, *, tq):
    bsz, s, _ = fq.shape
    grid = (bsz, FOX_HEADS // 2, s // tq)
    return pl.pallas_call(
        functools.partial(_fox_kernel, tq=tq),
        out_shape=jax.ShapeDtypeStruct((bsz, s, FOX_W), BF16),
        grid=grid,
        in_specs=[pl.BlockSpec((1, tq, LANES), lambda b, hp, i: (b, i, hp)),
                  pl.BlockSpec((1, s, LANES), lambda b, hp, i: (b, 0, hp)),
                  pl.BlockSpec((1, s, LANES), lambda b, hp, i: (b, 0, hp)),
                  pl.BlockSpec((1, 1, 2, s), lambda b, hp, i: (b, hp, 0, 0)),
                  pl.BlockSpec((1, PREFIX, LANES), lambda b, hp, i: (0, 0, hp)),
                  pl.BlockSpec((1, PREFIX, LANES), lambda b, hp, i: (0, 0, hp)),
                  pl.BlockSpec((1, 2, PREFIX), lambda b, hp, i: (hp, 0, 0)),
                  pl.BlockSpec((1, LANES), lambda b, hp, i: (0, hp))],
        out_specs=pl.BlockSpec((1, tq, LANES), lambda b, hp, i: (b, i, hp)),
        scratch_shapes=[pltpu.VMEM((tq, 1), F32), pltpu.VMEM((tq, 1), F32),
                        pltpu.VMEM((tq, LANES), F32)],
        compiler_params=pltpu.CompilerParams(
            dimension_semantics=("parallel", "parallel", "parallel")),
        name="fox",
    )(fq, fk, fv, cx, fk_pre, fv_pre, cmeta, gn)


def _out_proj_kernel(oa_ref, ob_ref, x_ref, eg_ref, eb_ref, w_ref, g_ref, b_ref, h1_ref):
    o = jnp.concatenate([oa_ref[0], ob_ref[0]], axis=-1)
    mix = _dot(o, w_ref[...])
    h0 = _layer_norm(x_ref[0], eg_ref[...], eb_ref[...])
    h1_ref[0] = _layer_norm(DN_ALPHA * h0 + mix, g_ref[...], b_ref[...])


def _out_proj(oa, ob, x, eg, eb, w, g, b, *, tm):
    bsz, s, d = x.shape
    grid = (bsz, s // tm)
    tok = lambda w_: pl.BlockSpec((1, tm, w_), lambda bb, i: (bb, i, 0))
    full = lambda a: pl.BlockSpec(a.shape, lambda bb, i: (0,) * a.ndim)
    return pl.pallas_call(
        _out_proj_kernel,
        out_shape=jax.ShapeDtypeStruct((bsz, s, d), F32),
        grid=grid,
        in_specs=[tok(GLA_V), tok(FOX_W), tok(d), full(eg), full(eb), full(w), full(g), full(b)],
        out_specs=tok(d),
        compiler_params=pltpu.CompilerParams(
            dimension_semantics=("parallel", "parallel"), vmem_limit_bytes=VMEM_LIMIT),
        name="out_proj",
    )(oa, ob, x, eg, eb, w, g, b)


def _extract_top(cur, iota, n_pick, idx_big):
    rank = jnp.full(cur.shape, float(n_pick), F32)
    vals = []
    for a in range(n_pick):
        m = jnp.max(cur, axis=0, keepdims=True)
        idx = jnp.min(jnp.where(cur == m, iota, idx_big), axis=0, keepdims=True)
        sel = iota == idx
        vals.append(m)
        rank = jnp.where(sel, float(a), rank)
        cur = jnp.where(sel, -jnp.inf, cur)
    return vals, rank


def _route_kernel(h_ref, wq_ref, keys_ref, lim_ref, e0_ref, rj_ref, e1_ref,
                  q_scr, cand_scr, sel_scr, *, tm):
    q_scr[...] = _dot(h_ref[...].astype(BF16), wq_ref[...]).astype(BF16)
    cand_scr[...] = jnp.full(cand_scr.shape, -jnp.inf, F32)
    iota_n = lax.broadcasted_iota(jnp.int32, (PEER_NKEYS, LANES), 0)
    iota_c = lax.broadcasted_iota(jnp.int32, (_N_CAND, LANES), 0)
    row_start = [next(r for r, (a, _) in enumerate(_PAIRS) if a == aa) for aa in range(PEER_TOPK)]
    row_end = row_start[1:] + [len(_PAIRS)]

    def per_tile(u, carry_u):
        ts = pl.multiple_of(u * LANES, LANES)

        def per_head(h, carry_h):
            halves = []
            for c in range(2):
                col = pl.multiple_of((2 * h + c) * PEER_HALF, PEER_HALF)
                qhc = q_scr[pl.ds(ts, LANES), pl.ds(col, PEER_HALF)]
                s = _dot_nt(keys_ref[2 * h + c], qhc)
                vals, rank = _extract_top(s, iota_n, PEER_TOPK, PEER_NKEYS)
                halves.append((s, vals, rank))
            (s0, tv0, rank0), (s1, tv1, rank1) = halves
            for r, (a, b) in enumerate(_PAIRS):
                cand_scr[r:r + 1, :] = tv0[a] + tv1[b]
            cand = cand_scr[...]
            _, crank = _extract_top(cand, iota_c, PEER_TOPK, _N_CAND)
            picked = jnp.where(crank < PEER_TOPK, 1.0, 0.0)
            z = jnp.sum(picked * jnp.exp(cand - cand[0:1, :]), axis=0, keepdims=True)
            sel_scr[...] = picked
            lim = jnp.zeros((PEER_NKEYS, LANES), F32)
            for a in range(PEER_TOPK):
                nb = jnp.sum(sel_scr[row_start[a]:row_end[a], :], axis=0, keepdims=True)
                lim = jnp.where(rank0 == float(a), nb, lim)
            e0 = jnp.where(rank0 < PEER_TOPK, jnp.exp(s0 - tv0[0]) * (1.0 / z), 0.0)
            e1 = jnp.where(rank1 < PEER_TOPK, jnp.exp(s1 - tv1[0]), 0.0)
            lim_ref[h, :, pl.ds(ts, LANES)] = lim
            e0_ref[h, :, pl.ds(ts, LANES)] = e0
            rj_ref[h, :, pl.ds(ts, LANES)] = rank1
            e1_ref[h, :, pl.ds(ts, LANES)] = e1
            return carry_h

        lax.fori_loop(0, PEER_HEADS, per_head, 0)
        return carry_u

    lax.fori_loop(0, tm // LANES, per_tile, 0)


def _route(h1, wq, keys, *, tm):
    t, d = h1.shape
    side = jax.ShapeDtypeStruct((PEER_HEADS, PEER_NKEYS, t), F32)
    side_spec = pl.BlockSpec((PEER_HEADS, PEER_NKEYS, tm), lambda i: (0, 0, i))
    return pl.pallas_call(
        functools.partial(_route_kernel, tm=tm),
        out_shape=(side, side, side, side),
        grid=(t // tm,),
        in_specs=[pl.BlockSpec((tm, d), lambda i: (i, 0)),
                  pl.BlockSpec(wq.shape, lambda i: (0, 0)),
                  pl.BlockSpec(keys.shape, lambda i: (0, 0, 0))],
        out_specs=(side_spec, side_spec, side_spec, side_spec),
        scratch_shapes=[pltpu.VMEM((tm, 2 * PEER_HEADS * PEER_HALF), BF16),
                        pltpu.VMEM((_N_CAND, LANES), F32),
                        pltpu.VMEM((_N_CAND, LANES), F32)],
        compiler_params=pltpu.CompilerParams(
            dimension_semantics=("parallel",), vmem_limit_bytes=VMEM_LIMIT),
        name="route",
    )(h1, wq, keys)


def _peer_kernel(h_ref, u_ref, vt_ref, lim_ref, e0_ref, rj_ref, e1_ref, g_ref, b_ref, o_ref,
                 hb_scr, p_scr, acc_scr, *, ni):
    ib = pl.program_id(1)

    @pl.when(ib == 0)
    def _():
        hb_scr[...] = h_ref[...].astype(BF16)
        acc_scr[...] = jnp.zeros(acc_scr.shape, F32)

    at = _dot_nt(u_ref[...], hb_scr[...])
    act = 0.5 * at * (1.0 + lax.erf(at * (2.0 ** -0.5)))
    for ii in range(ni):
        gate = jnp.zeros((PEER_NKEYS, at.shape[1]), F32)
        for h in range(PEER_HEADS):
            hit = rj_ref[h] < lim_ref[h, ii:ii + 1, :]
            gate = gate + jnp.where(hit, e1_ref[h] * e0_ref[h, ii:ii + 1, :], 0.0)
        rows = slice(ii * PEER_NKEYS, (ii + 1) * PEER_NKEYS)
        p_scr[rows, :] = (gate * act[rows, :]).astype(BF16)
    acc_scr[...] += _dot(vt_ref[...], p_scr[...])

    @pl.when(ib == pl.num_programs(1) - 1)
    def _():
        y = acc_scr[...].T
        o_ref[...] = _layer_norm(DN_ALPHA * h_ref[...] + y, g_ref[...], b_ref[...])


def _peer(h1, u, vt, lim, e0, rj, e1, g, b, *, tm, ni):
    t, d = h1.shape
    n_exp = u.shape[0]
    te = ni * PEER_NKEYS
    grid = (t // tm, n_exp // te)
    return pl.pallas_call(
        functools.partial(_peer_kernel, ni=ni),
        out_shape=jax.ShapeDtypeStruct((t, d), F32),
        grid=grid,
        in_specs=[pl.BlockSpec((tm, d), lambda i, j: (i, 0)),
                  pl.BlockSpec((te, d), lambda i, j: (j, 0)),
                  pl.BlockSpec((d, te), lambda i, j: (0, j)),
                  pl.BlockSpec((PEER_HEADS, ni, tm), lambda i, j: (0, j, i)),
                  pl.BlockSpec((PEER_HEADS, ni, tm), lambda i, j: (0, j, i)),
                  pl.BlockSpec((PEER_HEADS, PEER_NKEYS, tm), lambda i, j: (0, 0, i)),
                  pl.BlockSpec((PEER_HEADS, PEER_NKEYS, tm), lambda i, j: (0, 0, i)),
                  pl.BlockSpec(g.shape, lambda i, j: (0, 0)),
                  pl.BlockSpec(b.shape, lambda i, j: (0, 0))],
        out_specs=pl.BlockSpec((tm, d), lambda i, j: (i, 0)),
        scratch_shapes=[pltpu.VMEM((tm, d), BF16), pltpu.VMEM((te, tm), BF16),
                        pltpu.VMEM((d, tm), F32)],
        compiler_params=pltpu.CompilerParams(
            dimension_semantics=("parallel", "arbitrary"), vmem_limit_bytes=VMEM_LIMIT),
        name="peer",
    )(h1, u, vt, lim, e0, rj, e1, g, b)


def _pick_tile(n, pref):
    t = pref
    while n % t:
        t //= 2
    return t


def kernel(x, meta_tokens, emb_ln_g, emb_ln_b, w_in, w_gate_up, b_gate, b_forget, gla_norm_g,
           fox_norm_g, w_out, ln1_g, ln1_b, peer_w_q, peer_sub_keys, peer_u, peer_v, ln2_g, ln2_b):
    bsz, s, d = x.shape
    assert d == D_MODEL and s % (2 * GLA_CHUNK) == 0 and w_in.shape[0] == 1
    row = lambda v: v.reshape(1, -1).astype(F32)

    w0 = w_in[0]
    o_ga = 2 * GLA_QK + 2 * GLA_V
    o_fb = o_ga + GLA_RANK + 3 * FOX_W
    wm = jnp.concatenate([w0[:, :o_ga], w0[:, o_ga + GLA_RANK:o_fb]], axis=1).astype(BF16)
    wga = jnp.pad(w0[:, o_ga:o_ga + GLA_RANK], ((0, 0), (0, LANES - GLA_RANK))).astype(BF16)
    wft = jnp.pad(w0[:, o_fb:].T, ((0, 16 - FOX_HEADS), (0, 0))).astype(BF16)
    wup = jnp.pad(w_gate_up[0], ((0, LANES - GLA_RANK), (0, 0))).astype(BF16)
    bg = row(b_gate[0])
    bf = jnp.pad(b_forget[0].astype(F32), (0, 16 - FOX_HEADS)).reshape(16, 1)
    eg, eb = row(emb_ln_g), row(emb_ln_b)

    x_pre = jnp.concatenate([jnp.zeros((N_PAD, d), x.dtype), meta_tokens.astype(x.dtype)], axis=0)[None]
    pre = _in_proj(x_pre, eg, eb, wm, wga, wft, wup, bg, bf, tm=PREFIX)
    main = _in_proj(x, eg, eb, wm, wga, wft, wup, bg, bf, tm=_pick_tile(s, 256))
    gq_p, gk_p, gv_p, gr_p, gl_p, _, fk_p, fv_p, lf_p = pre
    gq, gk, gv, gr, gl, fq, fk, fv, lf = main

    gn = row(gla_norm_g[0])
    s_zero = jnp.zeros((2, 2 * GLA_CHUNK, GLA_DV), F32)
    _, s_pre = _gla(gq_p, gk_p, gv_p, gr_p, gl_p, gn, s_zero, n_invalid=N_PAD)
    oa, _ = _gla(gq, gk, gv, gr, gl, gn, s_pre[0], n_invalid=0)

    cmeta = _cumsum_lanes(lf_p[0], rel_end=True).reshape(8, 2, PREFIX)
    cx = _cumsum_lanes(lf.reshape(bsz * 16, s), rel_end=False).reshape(bsz, 8, 2, s)
    ob = _fox(fq, fk, fv, cx, fk_p, fv_p, cmeta, row(fox_norm_g[0]), tq=_pick_tile(s, 256))

    h1 = _out_proj(oa, ob, x, eg, eb, w_out[0].astype(BF16), row(ln1_g[0]), row(ln1_b[0]),
                   tm=_pick_tile(s, 512))

    t = bsz * s
    h1f = h1.reshape(t, d)
    keys = peer_sub_keys[0].reshape(2 * PEER_HEADS, PEER_NKEYS, PEER_HALF).astype(BF16)
    lim, e0, rj, e1 = _route(h1f, peer_w_q[0].astype(BF16), keys, tm=_pick_tile(t, 512))
    out = _peer(h1f, peer_u[0].astype(BF16), peer_v[0].astype(BF16).T, lim, e0, rj, e1,
                row(ln2_g[0]), row(ln2_b[0]), tm=_pick_tile(t, 512), ni=8)
    return out.reshape(bsz, s, d)
```

```python
import functools

import jax
import jax.numpy as jnp
from jax import lax
from jax.experimental import pallas as pl
from jax.experimental.pallas import tpu as pltpu

F32 = jnp.float32
BF16 = jnp.bfloat16

D_MODEL = 1024
N_META = 16
PREFIX = 128
N_PAD = PREFIX - N_META

GLA_HEADS = 4
GLA_DK = 64
GLA_DV = 128
GLA_RANK = 16
GLA_TAU = 16.0
GLA_CHUNK = 64
GLA_QK = GLA_HEADS * GLA_DK
GLA_V = GLA_HEADS * GLA_DV

FOX_HEADS = 8
FOX_DH = 64
FOX_W = FOX_HEADS * FOX_DH

PEER_HEADS = 8
PEER_NKEYS = 128
PEER_TOPK = 16
PEER_HALF = 128

DN_ALPHA = 2.0 ** 0.25
LN_EPS = 1e-5
NEG = -1e30

LANES = 128
VMEM_LIMIT = 48 * 1024 * 1024

_PAIRS = tuple((a, b) for a in range(PEER_TOPK) for b in range(PEER_TOPK)
               if (a + 1) * (b + 1) <= PEER_TOPK)
_N_CAND = 56


def _dot(a, b):
    return jnp.dot(a, b, preferred_element_type=F32)


def _dot_nt(a, b):
    return lax.dot_general(a, b, (((1,), (1,)), ((), ())), preferred_element_type=F32)


def _layer_norm(x, g, b):
    mu = jnp.mean(x, axis=-1, keepdims=True)
    xc = x - mu
    var = jnp.mean(xc * xc, axis=-1, keepdims=True)
    return xc * lax.rsqrt(var + LN_EPS) * g + b


def _split3(x):
    hi = x.astype(BF16)
    r1 = x - hi.astype(F32)
    mid = r1.astype(BF16)
    lo = (r1 - mid.astype(F32)).astype(BF16)
    return hi, mid, lo


def _in_proj_kernel(x_ref, g_ref, b_ref, wm_ref, wga_ref, wft_ref, wup_ref, bg_ref, bf_ref,
                    gq_ref, gk_ref, gv_ref, gr_ref, glog_ref, fq_ref, fk_ref, fv_ref, lf_ref):
    hn = _layer_norm(x_ref[0], g_ref[...], b_ref[...])
    hb = hn.astype(BF16)
    proj = _dot(hb, wm_ref[...])
    gq_ref[0] = proj[:, 0:256]
    gk_ref[0] = proj[:, 256:512]
    gv_ref[0] = proj[:, 512:1024].astype(BF16)
    gr_ref[0] = proj[:, 1024:1536]
    fq_ref[0] = (proj[:, 1536:2048] * (FOX_DH ** -0.5)).astype(BF16)
    fk_ref[0] = proj[:, 2048:2560].astype(BF16)
    fv_ref[0] = proj[:, 2560:3072].astype(BF16)
    ga = _dot(hb, wga_ref[...])
    z = _dot(ga.astype(BF16), wup_ref[...]) + bg_ref[...]
    glog_ref[0] = jax.nn.log_sigmoid(z) / GLA_TAU
    fbt = _dot_nt(wft_ref[...], hb)
    lf_ref[0] = jax.nn.log_sigmoid(fbt + bf_ref[...])


def _in_proj(x, ln_g, ln_b, wm, wga, wft, wup, bg, bf, *, tm):
    bsz, s, d = x.shape
    grid = (bsz, s // tm)
    tok = lambda w: pl.BlockSpec((1, tm, w), lambda b, i: (b, i, 0))
    full = lambda a: pl.BlockSpec(a.shape, lambda b, i: (0,) * a.ndim)
    out_shape = (
        jax.ShapeDtypeStruct((bsz, s, GLA_QK), F32),
        jax.ShapeDtypeStruct((bsz, s, GLA_QK), F32),
        jax.ShapeDtypeStruct((bsz, s, GLA_V), BF16),
        jax.ShapeDtypeStruct((bsz, s, GLA_V), F32),
        jax.ShapeDtypeStruct((bsz, s, GLA_QK), F32),
        jax.ShapeDtypeStruct((bsz, s, FOX_W), BF16),
        jax.ShapeDtypeStruct((bsz, s, FOX_W), BF16),
        jax.ShapeDtypeStruct((bsz, s, FOX_W), BF16),
        jax.ShapeDtypeStruct((bsz, 16, s), F32),
    )
    out_specs = (tok(GLA_QK), tok(GLA_QK), tok(GLA_V), tok(GLA_V), tok(GLA_QK),
                 tok(FOX_W), tok(FOX_W), tok(FOX_W),
                 pl.BlockSpec((1, 16, tm), lambda b, i: (b, 0, i)))
    return pl.pallas_call(
        _in_proj_kernel,
        out_shape=out_shape,
        grid=grid,
        in_specs=[tok(d), full(ln_g), full(ln_b), full(wm), full(wga), full(wft), full(wup),
                  full(bg), full(bf)],
        out_specs=out_specs,
        compiler_params=pltpu.CompilerParams(
            dimension_semantics=("parallel", "parallel"), vmem_limit_bytes=VMEM_LIMIT),
        name="in_proj",
    )(x, ln_g, ln_b, wm, wga, wft, wup, bg, bf)


def _cumsum_kernel(x_ref, o_ref, *, n_tiles, rel_end):
    ri = lax.broadcasted_iota(jnp.int32, (LANES, LANES), 0)
    ci = lax.broadcasted_iota(jnp.int32, (LANES, LANES), 1)
    upper = (ri <= ci).astype(BF16)
    carry = jnp.zeros((x_ref.shape[0], 1), F32)
    for j in range(n_tiles):
        sl = slice(j * LANES, (j + 1) * LANES)
        hi, mid, lo = _split3(x_ref[:, sl])
        cs = (_dot(hi, upper) + _dot(mid, upper)) + _dot(lo, upper) + carry
        o_ref[:, sl] = cs
        carry = cs[:, LANES - 1:LANES]
    if rel_end:
        o_ref[...] = o_ref[...] - carry


def _cumsum_lanes(x, *, rel_end):
    rows, n = x.shape
    return pl.pallas_call(
        functools.partial(_cumsum_kernel, n_tiles=n // LANES, rel_end=rel_end),
        out_shape=jax.ShapeDtypeStruct((rows, n), F32),
        name="cumsum",
    )(x)


def _gla_kernel(q_ref, k_ref, v_ref, r_ref, gl_ref, gn_ref, s0_ref, o_ref, sf_ref, s_scr,
                *, n_invalid):
    t = pl.program_id(1)

    @pl.when(t == 0)
    def _():
        s_scr[...] = s0_ref[...]

    tl = 2 * GLA_CHUNK
    ri = lax.broadcasted_iota(jnp.int32, (tl, tl), 0)
    ci = lax.broadcasted_iota(jnp.int32, (tl, tl), 1)
    same_chunk = (ri // GLA_CHUNK) == (ci // GLA_CHUNK)
    tri = ((ci <= ri) & same_chunk)
    tri_b = tri.astype(BF16)
    first = ri < GLA_CHUNK
    lane = lax.broadcasted_iota(jnp.int32, (1, tl), 1)
    row1 = lax.broadcasted_iota(jnp.int32, (tl, 1), 0)
    scale = GLA_DK ** -0.5
    valid = (row1 + t * tl) >= n_invalid

    for p in range(GLA_HEADS // 2):
        cs = slice(p * tl, (p + 1) * tl)
        g = gl_ref[0, :, cs]
        hi, mid, lo = _split3(g)
        bcum = (_dot(tri_b, hi) + _dot(tri_b, mid)) + _dot(tri_b, lo)
        half = GLA_CHUNK // 2
        bref = jnp.where(first, bcum[half - 1:half, :], bcum[GLA_CHUNK + half - 1:GLA_CHUNK + half, :])
        blast = jnp.where(first, bcum[GLA_CHUNK - 1:GLA_CHUNK, :], bcum[tl - 1:tl, :])
        q = q_ref[0, :, cs]
        k = jnp.where(valid, k_ref[0, :, cs], 0.0)
        q_in = q * jnp.exp(bcum - bref)
        k_in = (k * jnp.exp(bref - bcum)).astype(BF16)
        qe = q * jnp.exp(bcum)
        kdt = (k * jnp.exp(blast - bcum)).T
        bcum_t = bcum.T
        dec0 = jnp.exp(bcum_t[:, GLA_CHUNK - 1:GLA_CHUNK])
        dec1 = jnp.exp(bcum_t[:, tl - 1:tl])
        kdt0 = jnp.where(lane < GLA_CHUNK, kdt, 0.0).astype(BF16)
        kdt1 = jnp.where(lane >= GLA_CHUNK, kdt, 0.0).astype(BF16)

        v_h = [v_ref[0, :, (2 * p + hh) * GLA_DV:(2 * p + hh + 1) * GLA_DV] for hh in range(2)]
        s_in = s_scr[p]
        ds0 = jnp.where(first, _dot(kdt0, v_h[0]), _dot(kdt0, v_h[1]))
        ds1 = jnp.where(first, _dot(kdt1, v_h[0]), _dot(kdt1, v_h[1]))
        s_mid = dec0 * s_in + ds0
        s_out = dec1 * s_mid + ds1
        s_scr[p] = s_out
        sf_ref[0, p] = s_out
        s_in_b = s_in.astype(BF16)
        s_mid_b = s_mid.astype(BF16)

        for hh in range(2):
            h = 2 * p + hh
            mh = (lane // GLA_DK) == hh
            a = _dot_nt(jnp.where(mh, q_in, 0.0).astype(BF16), k_in) * scale
            a = jnp.where(tri, a, 0.0)
            o_intra = _dot(a.astype(BF16), v_h[hh])
            qe_m = jnp.where(mh, qe, 0.0).astype(BF16)
            o_inter = jnp.where(row1 < GLA_CHUNK, _dot(qe_m, s_in_b), _dot(qe_m, s_mid_b)) * scale
            o = o_intra + o_inter
            hs = slice(h * GLA_DV, (h + 1) * GLA_DV)
            rn = o * lax.rsqrt(jnp.mean(o * o, axis=-1, keepdims=True) + LN_EPS) * gn_ref[:, hs]
            ra = r_ref[0, :, hs]
            o_ref[0, :, hs] = (rn * (ra * jax.nn.sigmoid(ra))).astype(BF16)


def _gla(gq, gk, gv, gr, glog, gn, s0, *, n_invalid):
    bsz, s, _ = gq.shape
    tl = 2 * GLA_CHUNK
    grid = (bsz, s // tl)
    tok = lambda w: pl.BlockSpec((1, tl, w), lambda b, t: (b, t, 0))
    return pl.pallas_call(
        functools.partial(_gla_kernel, n_invalid=n_invalid),
        out_shape=(jax.ShapeDtypeStruct((bsz, s, GLA_V), BF16),
                   jax.ShapeDtypeStruct((bsz, 2, tl, GLA_DV), F32)),
        grid=grid,
        in_specs=[tok(GLA_QK), tok(GLA_QK), tok(GLA_V), tok(GLA_V), tok(GLA_QK),
                  pl.BlockSpec((1, GLA_V), lambda b, t: (0, 0)),
                  pl.BlockSpec((2, tl, GLA_DV), lambda b, t: (0, 0, 0))],
        out_specs=(tok(GLA_V), pl.BlockSpec((1, 2, tl, GLA_DV), lambda b, t: (b, 0, 0, 0))),
        scratch_shapes=[pltpu.VMEM((2, tl, GLA_DV), F32)],
        compiler_params=pltpu.CompilerParams(dimension_semantics=("parallel", "arbitrary")),
        name="gla",
    )(gq, gk, gv, gr, glog, gn, s0)


def _fox_kernel(q_ref, k_ref, v_ref, cx_ref, kp_ref, vp_ref, cm_ref, g_ref, o_ref,
                m_scr, l_scr, acc_scr, *, tq):
    qi = pl.program_id(2)
    q = q_ref[0]
    lane = lax.broadcasted_iota(jnp.int32, (1, LANES), 1)
    qstart = pl.multiple_of(qi * tq, tq)
    ri = lax.broadcasted_iota(jnp.int32, (tq, tq), 0)
    ci = lax.broadcasted_iota(jnp.int32, (tq, tq), 1)

    def update(s, vblk):
        m_prev = m_scr[...]
        m_new = jnp.maximum(m_prev, jnp.max(s, axis=-1, keepdims=True))
        alpha = jnp.exp(m_prev - m_new)
        p = jnp.exp(s - m_new)
        l_scr[...] = alpha * l_scr[...] + jnp.sum(p, axis=-1, keepdims=True)
        acc_scr[...] = alpha * acc_scr[...] + _dot(p.astype(BF16), vblk)
        m_scr[...] = m_new

    heads = []
    for hh in range(2):
        mh = (lane // FOX_DH) == hh
        qm = jnp.where(mh, q, jnp.zeros_like(q))
        cref = cx_ref[0, 0, hh:hh + 1, pl.ds(qstart, LANES)][:, 0:1]
        m_scr[...] = jnp.full(m_scr.shape, -jnp.inf, F32)
        l_scr[...] = jnp.zeros(l_scr.shape, F32)
        acc_scr[...] = jnp.zeros(acc_scr.shape, F32)

        s = _dot_nt(qm, kp_ref[0]) + (cref - cm_ref[0, hh:hh + 1, :])
        update(jnp.where(lane >= N_PAD, s, NEG), vp_ref[0])

        def body(j, carry):
            ks = pl.multiple_of(j * tq, tq)
            s = _dot_nt(qm, k_ref[0, pl.ds(ks, tq), :]) + (cref - cx_ref[0, 0, hh:hh + 1, pl.ds(ks, tq)])
            update(s, v_ref[0, pl.ds(ks, tq), :])
            return carry

        lax.fori_loop(0, qi, body, 0)

        s = _dot_nt(qm, k_ref[0, pl.ds(qstart, tq), :]) + (cref - cx_ref[0, 0, hh:hh + 1, pl.ds(qstart, tq)])
        update(jnp.where(ci <= ri, s, NEG), v_ref[0, pl.ds(qstart, tq), :])
        heads.append(acc_scr[...] / l_scr[...])

    first = lane < FOX_DH
    o = jnp.where(first, heads[0], heads[1])
    sq = o * o
    ms0 = jnp.sum(jnp.where(first, sq, 0.0), axis=-1, keepdims=True) / FOX_DH
    ms1 = jnp.sum(jnp.where(first, 0.0, sq), axis=-1, keepdims=True) / FOX_DH
    rs = jnp.where(first, lax.rsqrt(ms0 + LN_EPS), lax.rsqrt(ms1 + LN_EPS))
    o_ref[0] = (o * rs * g_ref[...]).astype(BF16)


def _fox(fq, fk, fv, cx, fk_pre, fv_pre, cmeta, gn, *, tq):
    bsz, s, _ = fq.shape
    grid = (bsz, FOX_HEADS // 2, s // tq)
    return pl.pallas_call(
        functools.partial(_fox_kernel, tq=tq),
        out_shape=jax.ShapeDtypeStruct((bsz, s, FOX_W), BF16),
        grid=grid,
        in_specs=[pl.BlockSpec((1, tq, LANES), lambda b, hp, i: (b, i, hp)),
                  pl.BlockSpec((1, s, LANES), lambda b, hp, i: (b, 0, hp)),
                  pl.BlockSpec((1, s, LANES), lambda b, hp, i: (b, 0, hp)),
                  pl.BlockSpec((1, 1, 2, s), lambda b, hp, i: (b, hp, 0, 0)),
                  pl.BlockSpec((1, PREFIX, LANES), lambda b, hp, i: (0, 0, hp)),
                  pl.BlockSpec((1, PREFIX, LANES), lambda b, hp, i: (0, 0, hp)),
                  pl.BlockSpec((1, 2, PREFIX), lambda b, hp, i: (hp, 0, 0)),
                  pl.BlockSpec((1, LANES), lambda b, hp, i: (0, hp))],
        out_specs=pl.BlockSpec((1, tq, LANES), lambda b, hp, i: (b, i, hp)),
        scratch_shapes=[pltpu.VMEM((tq, 1), F32), pltpu.VMEM((tq, 1), F32),
                        pltpu.VMEM((tq, LANES), F32)],
        compiler_params=pltpu.CompilerParams(
            dimension_semantics=("parallel", "parallel", "parallel")),
        name="fox",
    )(fq, fk, fv, cx, fk_pre, fv_pre, cmeta, gn)


def _out_proj_kernel(oa_ref, ob_ref, x_ref, eg_ref, eb_ref, w_ref, g_ref, b_ref, h1_ref):
    o = jnp.concatenate([oa_ref[0], ob_ref[0]], axis=-1)
    mix = _dot(o, w_ref[...])
    h0 = _layer_norm(x_ref[0], eg_ref[...], eb_ref[...])
    h1_ref[0] = _layer_norm(DN_ALPHA * h0 + mix, g_ref[...], b_ref[...])


def _out_proj(oa, ob, x, eg, eb, w, g, b, *, tm):
    bsz, s, d = x.shape
    grid = (bsz, s // tm)
    tok = lambda w_: pl.BlockSpec((1, tm, w_), lambda bb, i: (bb, i, 0))
    full = lambda a: pl.BlockSpec(a.shape, lambda bb, i: (0,) * a.ndim)
    return pl.pallas_call(
        _out_proj_kernel,
        out_shape=jax.ShapeDtypeStruct((bsz, s, d), F32),
        grid=grid,
        in_specs=[tok(GLA_V), tok(FOX_W), tok(d), full(eg), full(eb), full(w), full(g), full(b)],
        out_specs=tok(d),
        compiler_params=pltpu.CompilerParams(
            dimension_semantics=("parallel", "parallel"), vmem_limit_bytes=VMEM_LIMIT),
        name="out_proj",
    )(oa, ob, x, eg, eb, w, g, b)


def _extract_top(cur, iota, n_pick, idx_big):
    rank = jnp.full(cur.shape, float(n_pick), F32)
    vals = []
    for a in range(n_pick):
        m = jnp.max(cur, axis=0, keepdims=True)
        idx = jnp.min(jnp.where(cur == m, iota, idx_big), axis=0, keepdims=True)
        sel = iota == idx
        vals.append(m)
        rank = jnp.where(sel, float(a), rank)
        cur = jnp.where(sel, -jnp.inf, cur)
    return vals, rank


def _route_kernel(h_ref, wq_ref, keys_ref, lim_ref, e0_ref, rj_ref, e1_ref,
                  q_scr, cand_scr, sel_scr, *, tm):
    q_scr[...] = _dot(h_ref[...].astype(BF16), wq_ref[...]).astype(BF16)
    cand_scr[...] = jnp.full(cand_scr.shape, -jnp.inf, F32)
    iota_n = lax.broadcasted_iota(jnp.int32, (PEER_NKEYS, LANES), 0)
    iota_c = lax.broadcasted_iota(jnp.int32, (_N_CAND, LANES), 0)
    row_start = [next(r for r, (a, _) in enumerate(_PAIRS) if a == aa) for aa in range(PEER_TOPK)]
    row_end = row_start[1:] + [len(_PAIRS)]

    def per_tile(u, carry_u):
        ts = pl.multiple_of(u * LANES, LANES)

        def per_head(h, carry_h):
            halves = []
            for c in range(2):
                col = pl.multiple_of((2 * h + c) * PEER_HALF, PEER_HALF)
                qhc = q_scr[pl.ds(ts, LANES), pl.ds(col, PEER_HALF)]
                s = _dot_nt(keys_ref[2 * h + c], qhc)
                vals, rank = _extract_top(s, iota_n, PEER_TOPK, PEER_NKEYS)
                halves.append((s, vals, rank))
            (s0, tv0, rank0), (s1, tv1, rank1) = halves
            for r, (a, b) in enumerate(_PAIRS):
                cand_scr[r:r + 1, :] = tv0[a] + tv1[b]
            cand = cand_scr[...]
            _, crank = _extract_top(cand, iota_c, PEER_TOPK, _N_CAND)
            picked = jnp.where(crank < PEER_TOPK, 1.0, 0.0)
            z = jnp.sum(picked * jnp.exp(cand - cand[0:1, :]), axis=0, keepdims=True)
            sel_scr[...] = picked
            lim = jnp.zeros((PEER_NKEYS, LANES), F32)
            for a in range(PEER_TOPK):
                nb = jnp.sum(sel_scr[row_start[a]:row_end[a], :], axis=0, keepdims=True)
                lim = jnp.where(rank0 == float(a), nb, lim)
            e0 = jnp.where(rank0 < PEER_TOPK, jnp.exp(s0 - tv0[0]) * (1.0 / z), 0.0)
            e1 = jnp.where(rank1 < PEER_TOPK, jnp.exp(s1 - tv1[0]), 0.0)
            lim_ref[h, :, pl.ds(ts, LANES)] = lim
            e0_ref[h, :, pl.ds(ts, LANES)] = e0
            rj_ref[h, :, pl.ds(ts, LANES)] = _to_words(rank1)
            e1_ref[h, :, pl.ds(ts, LANES)] = _to_words(e1)
            return carry_h

        lax.fori_loop(0, PEER_HEADS, per_head, 0)
        return carry_u

    lax.fori_loop(0, tm // LANES, per_tile, 0)


def _route(h1, wq, keys, *, tm):
    t, d = h1.shape
    side = jax.ShapeDtypeStruct((PEER_HEADS, PEER_NKEYS, t), F32)
    nw = PEER_NKEYS // _pack_factor()
    side_w = jax.ShapeDtypeStruct((PEER_HEADS, nw, t), jnp.uint32)
    side_spec = pl.BlockSpec((PEER_HEADS, PEER_NKEYS, tm), lambda i: (0, 0, i))
    side_w_spec = pl.BlockSpec((PEER_HEADS, nw, tm), lambda i: (0, 0, i))
    return pl.pallas_call(
        functools.partial(_route_kernel, tm=tm),
        out_shape=(side, side, side_w, side_w),
        grid=(t // tm,),
        in_specs=[pl.BlockSpec((tm, d), lambda i: (i, 0)),
                  pl.BlockSpec(wq.shape, lambda i: (0, 0)),
                  pl.BlockSpec(keys.shape, lambda i: (0, 0, 0))],
        out_specs=(side_spec, side_spec, side_w_spec, side_w_spec),
        scratch_shapes=[pltpu.VMEM((tm, 2 * PEER_HEADS * PEER_HALF), BF16),
                        pltpu.VMEM((_N_CAND, LANES), F32),
                        pltpu.VMEM((_N_CAND, LANES), F32)],
        compiler_params=pltpu.CompilerParams(
            dimension_semantics=("parallel",), vmem_limit_bytes=VMEM_LIMIT),
        name="route",
    )(h1, wq, keys)


SUBLANES = 8
_JB = 2
_IB = 4


def _pack_factor():
    return 4 // jnp.dtype(BF16).itemsize


def _to_words(x):
    return pltpu.bitcast(x.astype(BF16), jnp.uint32)


def _from_words(w):
    return pltpu.bitcast(w, BF16)


def _peer_kernel(h_ref, u_ref, vt_ref, lim_ref, e0_ref, rj_ref, e1_ref, g_ref, b_ref, o_ref,
                 hb_scr, act_scr, limb_scr, e0b_scr, p_scr, acc_scr, *, ni, tm, sub):
    ib = pl.program_id(1)
    pk = _pack_factor()
    rows = SUBLANES * pk

    @pl.when(ib == 0)
    def _():
        hb_scr[...] = h_ref[...].astype(BF16)
        acc_scr[...] = jnp.zeros(acc_scr.shape, F32)

    for h in range(PEER_HEADS):
        for ii in range(ni):
            limb_scr[h, ii] = _to_words(jnp.broadcast_to(lim_ref[h, ii:ii + 1, :], (rows, tm)))
            e0b_scr[h, ii] = _to_words(jnp.broadcast_to(e0_ref[h, ii:ii + 1, :], (rows, tm)))

    nst = tm // sub

    def activations(st):
        cs = slice(st * sub, (st + 1) * sub)
        at = _dot_nt(u_ref[...], hb_scr[cs, :])
        act_scr[st] = _to_words(0.5 * at * (1.0 + lax.erf(at * (2.0 ** -0.5))))

    def gates(st):
        cs = slice(st * sub, (st + 1) * sub)
        for lt in range(st * (sub // LANES), (st + 1) * (sub // LANES)):
            ls = slice(lt * LANES, (lt + 1) * LANES)
            for jb0 in range(0, PEER_NKEYS // rows, _JB):
                for ii0 in range(0, ni, _IB):
                    gate = [[jnp.zeros((rows, LANES), BF16) for _ in range(_IB)] for _ in range(_JB)]
                    for h in range(PEER_HEADS):
                        js = [slice((jb0 + a) * SUBLANES, (jb0 + a + 1) * SUBLANES) for a in range(_JB)]
                        rjv = [_from_words(rj_ref[h, js[a], ls]) for a in range(_JB)]
                        e1v = [_from_words(e1_ref[h, js[a], ls]) for a in range(_JB)]
                        for b in range(_IB):
                            lm = _from_words(limb_scr[h, ii0 + b, :, ls])
                            ev = _from_words(e0b_scr[h, ii0 + b, :, ls])
                            for a in range(_JB):
                                prod = e1v[a] * ev
                                gate[a][b] = gate[a][b] + jnp.where(rjv[a] < lm, prod, jnp.zeros_like(prod))
                    for a in range(_JB):
                        for b in range(_IB):
                            w0 = (ii0 + b) * (PEER_NKEYS // pk) + (jb0 + a) * SUBLANES
                            ws = slice(w0, w0 + SUBLANES)
                            lo = slice(ls.start - cs.start, ls.stop - cs.start)
                            p_scr[st, ws, lo] = _to_words(gate[a][b] * _from_words(act_scr[st, ws, lo]))

    def accumulate(st):
        acc_scr[st] += _dot(vt_ref[...], _from_words(p_scr[st]))

    activations(0)
    for st in range(nst):
        if st + 1 < nst:
            activations(st + 1)
        if st > 0:
            accumulate(st - 1)
        gates(st)
    accumulate(nst - 1)

    @pl.when(ib == pl.num_programs(1) - 1)
    def _():
        for st in range(tm // sub):
            cs = slice(st * sub, (st + 1) * sub)
            y = acc_scr[st].T
            o_ref[cs, :] = _layer_norm(DN_ALPHA * h_ref[cs, :] + y, g_ref[...], b_ref[...])


def _peer(h1, u, vt, lim, e0, rj, e1, g, b, *, tm, ni):
    t, d = h1.shape
    n_exp = u.shape[0]
    te = ni * PEER_NKEYS
    grid = (t // tm, n_exp // te)
    pk = _pack_factor()
    words = lambda shape: pltpu.VMEM(shape, jnp.uint32)
    sub = min(tm, 2 * LANES)
    nst = tm // sub
    return pl.pallas_call(
        functools.partial(_peer_kernel, ni=ni, tm=tm, sub=sub),
        out_shape=jax.ShapeDtypeStruct((t, d), F32),
        grid=grid,
        in_specs=[pl.BlockSpec((tm, d), lambda i, j: (i, 0), pipeline_mode=pl.Buffered(1)),
                  pl.BlockSpec((te, d), lambda i, j: (j, 0)),
                  pl.BlockSpec((d, te), lambda i, j: (0, j)),
                  pl.BlockSpec((PEER_HEADS, ni, tm), lambda i, j: (0, j, i)),
                  pl.BlockSpec((PEER_HEADS, ni, tm), lambda i, j: (0, j, i)),
                  pl.BlockSpec((PEER_HEADS, PEER_NKEYS // pk, tm), lambda i, j: (0, 0, i),
                               pipeline_mode=pl.Buffered(1)),
                  pl.BlockSpec((PEER_HEADS, PEER_NKEYS // pk, tm), lambda i, j: (0, 0, i),
                               pipeline_mode=pl.Buffered(1)),
                  pl.BlockSpec(g.shape, lambda i, j: (0, 0)),
                  pl.BlockSpec(b.shape, lambda i, j: (0, 0))],
        out_specs=pl.BlockSpec((tm, d), lambda i, j: (i, 0)),
        scratch_shapes=[pltpu.VMEM((tm, d), BF16), words((nst, te // pk, sub)),
                        words((PEER_HEADS, ni, SUBLANES, tm)),
                        words((PEER_HEADS, ni, SUBLANES, tm)),
                        words((nst, te // pk, sub)), pltpu.VMEM((nst, d, sub), F32)],
        compiler_params=pltpu.CompilerParams(
            dimension_semantics=("parallel", "arbitrary"), vmem_limit_bytes=VMEM_LIMIT),
        name="peer",
    )(h1, u, vt, lim, e0, rj, e1, g, b)


def _pick_tile(n, pref):
    t = pref
    while n % t:
        t //= 2
    return t


def kernel(x, meta_tokens, emb_ln_g, emb_ln_b, w_in, w_gate_up, b_gate, b_forget, gla_norm_g,
           fox_norm_g, w_out, ln1_g, ln1_b, peer_w_q, peer_sub_keys, peer_u, peer_v, ln2_g, ln2_b):
    bsz, s, d = x.shape
    assert d == D_MODEL and s % (2 * GLA_CHUNK) == 0 and w_in.shape[0] == 1
    row = lambda v: v.reshape(1, -1).astype(F32)

    w0 = w_in[0]
    o_ga = 2 * GLA_QK + 2 * GLA_V
    o_fb = o_ga + GLA_RANK + 3 * FOX_W
    wm = jnp.concatenate([w0[:, :o_ga], w0[:, o_ga + GLA_RANK:o_fb]], axis=1).astype(BF16)
    wga = jnp.pad(w0[:, o_ga:o_ga + GLA_RANK], ((0, 0), (0, LANES - GLA_RANK))).astype(BF16)
    wft = jnp.pad(w0[:, o_fb:].T, ((0, 16 - FOX_HEADS), (0, 0))).astype(BF16)
    wup = jnp.pad(w_gate_up[0], ((0, LANES - GLA_RANK), (0, 0))).astype(BF16)
    bg = row(b_gate[0])
    bf = jnp.pad(b_forget[0].astype(F32), (0, 16 - FOX_HEADS)).reshape(16, 1)
    eg, eb = row(emb_ln_g), row(emb_ln_b)

    x_pre = jnp.concatenate([jnp.zeros((N_PAD, d), x.dtype), meta_tokens.astype(x.dtype)], axis=0)[None]
    pre = _in_proj(x_pre, eg, eb, wm, wga, wft, wup, bg, bf, tm=PREFIX)
    main = _in_proj(x, eg, eb, wm, wga, wft, wup, bg, bf, tm=_pick_tile(s, 256))
    gq_p, gk_p, gv_p, gr_p, gl_p, _, fk_p, fv_p, lf_p = pre
    gq, gk, gv, gr, gl, fq, fk, fv, lf = main

    gn = row(gla_norm_g[0])
    s_zero = jnp.zeros((2, 2 * GLA_CHUNK, GLA_DV), F32)
    _, s_pre = _gla(gq_p, gk_p, gv_p, gr_p, gl_p, gn, s_zero, n_invalid=N_PAD)
    oa, _ = _gla(gq, gk, gv, gr, gl, gn, s_pre[0], n_invalid=0)

    cmeta = _cumsum_lanes(lf_p[0], rel_end=True).reshape(8, 2, PREFIX)
    cx = _cumsum_lanes(lf.reshape(bsz * 16, s), rel_end=False).reshape(bsz, 8, 2, s)
    ob = _fox(fq, fk, fv, cx, fk_p, fv_p, cmeta, row(fox_norm_g[0]), tq=_pick_tile(s, 256))

    h1 = _out_proj(oa, ob, x, eg, eb, w_out[0].astype(BF16), row(ln1_g[0]), row(ln1_b[0]),
                   tm=_pick_tile(s, 512))

    t = bsz * s
    h1f = h1.reshape(t, d)
    keys = peer_sub_keys[0].reshape(2 * PEER_HEADS, PEER_NKEYS, PEER_HALF).astype(BF16)
    lim, e0, rj, e1 = _route(h1f, peer_w_q[0].astype(BF16), keys, tm=_pick_tile(t, 512))
    out = _peer(h1f, peer_u[0].astype(BF16), peer_v[0].astype(BF16).T, lim, e0, rj, e1,
                row(ln2_g[0]), row(ln2_b[0]), tm=_pick_tile(t, 1024), ni=8)
    return out.reshape(bsz, s, d)
```

```python
import functools

import jax
import jax.numpy as jnp
from jax import lax
from jax.experimental import pallas as pl
from jax.experimental.pallas import tpu as pltpu

F32 = jnp.float32
BF16 = jnp.bfloat16

D_MODEL = 1024
N_META = 16
PREFIX = 128
N_PAD = PREFIX - N_META

GLA_HEADS = 4
GLA_DK = 64
GLA_DV = 128
GLA_RANK = 16
GLA_TAU = 16.0
GLA_CHUNK = 64
GLA_QK = GLA_HEADS * GLA_DK
GLA_V = GLA_HEADS * GLA_DV

FOX_HEADS = 8
FOX_DH = 64
FOX_W = FOX_HEADS * FOX_DH

PEER_HEADS = 8
PEER_NKEYS = 128
PEER_TOPK = 16
PEER_HALF = 128

DN_ALPHA = 2.0 ** 0.25
LN_EPS = 1e-5
NEG = -1e30

LANES = 128
VMEM_LIMIT = 48 * 1024 * 1024

_PAIRS = tuple((a, b) for a in range(PEER_TOPK) for b in range(PEER_TOPK)
               if (a + 1) * (b + 1) <= PEER_TOPK)
_N_CAND = 56


def _dot(a, b):
    return jnp.dot(a, b, preferred_element_type=F32)


def _dot_nt(a, b):
    return lax.dot_general(a, b, (((1,), (1,)), ((), ())), preferred_element_type=F32)


def _layer_norm(x, g, b):
    mu = jnp.mean(x, axis=-1, keepdims=True)
    xc = x - mu
    var = jnp.mean(xc * xc, axis=-1, keepdims=True)
    return xc * lax.rsqrt(var + LN_EPS) * g + b


def _split3(x):
    hi = x.astype(BF16)
    r1 = x - hi.astype(F32)
    mid = r1.astype(BF16)
    lo = (r1 - mid.astype(F32)).astype(BF16)
    return hi, mid, lo


def _in_proj_kernel(x_ref, g_ref, b_ref, wm_ref, wqv_ref, wga_ref, wft_ref, wup_ref, bg_ref, bf_ref,
                    gq_ref, gk_ref, gv_ref, gr_ref, glog_ref, fqt_ref, fk_ref, fvt_ref, lf_ref):
    hn = _layer_norm(x_ref[0], g_ref[...], b_ref[...])
    hb = hn.astype(BF16)
    proj = _dot(hb, wm_ref[...])
    gq_ref[0] = proj[:, 0:256]
    gk_ref[0] = proj[:, 256:512]
    gv_ref[0] = proj[:, 512:1024].astype(BF16)
    gr_ref[0] = proj[:, 1024:1536]
    fk_ref[0] = proj[:, 1536:2048].astype(BF16)
    qvt = _dot_nt(wqv_ref[...], hb)
    fqt_ref[0] = (qvt[0:FOX_W, :] * (FOX_DH ** -0.5)).astype(BF16)
    fvt_ref[0] = qvt[FOX_W:2 * FOX_W, :].astype(BF16)
    ga = _dot(hb, wga_ref[...])
    z = _dot(ga.astype(BF16), wup_ref[...]) + bg_ref[...]
    glog_ref[0] = jax.nn.log_sigmoid(z) / GLA_TAU
    fbt = _dot_nt(wft_ref[...], hb)
    lf_ref[0] = jax.nn.log_sigmoid(fbt + bf_ref[...])


def _in_proj(x, ln_g, ln_b, wm, wqv, wga, wft, wup, bg, bf, *, tm):
    bsz, s, d = x.shape
    grid = (bsz, s // tm)
    tok = lambda w: pl.BlockSpec((1, tm, w), lambda b, i: (b, i, 0))
    feat = lambda r: pl.BlockSpec((1, r, tm), lambda b, i: (b, 0, i))
    full = lambda a: pl.BlockSpec(a.shape, lambda b, i: (0,) * a.ndim)
    out_shape = (
        jax.ShapeDtypeStruct((bsz, s, GLA_QK), F32),
        jax.ShapeDtypeStruct((bsz, s, GLA_QK), F32),
        jax.ShapeDtypeStruct((bsz, s, GLA_V), BF16),
        jax.ShapeDtypeStruct((bsz, s, GLA_V), F32),
        jax.ShapeDtypeStruct((bsz, s, GLA_QK), F32),
        jax.ShapeDtypeStruct((bsz, FOX_W, s), BF16),
        jax.ShapeDtypeStruct((bsz, s, FOX_W), BF16),
        jax.ShapeDtypeStruct((bsz, FOX_W, s), BF16),
        jax.ShapeDtypeStruct((bsz, 16, s), F32),
    )
    out_specs = (tok(GLA_QK), tok(GLA_QK), tok(GLA_V), tok(GLA_V), tok(GLA_QK),
                 feat(FOX_W), tok(FOX_W), feat(FOX_W), feat(16))
    return pl.pallas_call(
        _in_proj_kernel,
        out_shape=out_shape,
        grid=grid,
        in_specs=[tok(d), full(ln_g), full(ln_b), full(wm), full(wqv), full(wga), full(wft),
                  full(wup), full(bg), full(bf)],
        out_specs=out_specs,
        compiler_params=pltpu.CompilerParams(
            dimension_semantics=("parallel", "parallel"), vmem_limit_bytes=VMEM_LIMIT),
        name="in_proj",
    )(x, ln_g, ln_b, wm, wqv, wga, wft, wup, bg, bf)


def _cumsum_kernel(x_ref, o_ref, *, n_tiles, rel_end):
    ri = lax.broadcasted_iota(jnp.int32, (LANES, LANES), 0)
    ci = lax.broadcasted_iota(jnp.int32, (LANES, LANES), 1)
    upper = (ri <= ci).astype(BF16)
    carry = jnp.zeros((x_ref.shape[0], 1), F32)
    for j in range(n_tiles):
        sl = slice(j * LANES, (j + 1) * LANES)
        hi, mid, lo = _split3(x_ref[:, sl])
        cs = (_dot(hi, upper) + _dot(mid, upper)) + _dot(lo, upper) + carry
        o_ref[:, sl] = cs
        carry = cs[:, LANES - 1:LANES]
    if rel_end:
        o_ref[...] = o_ref[...] - carry


def _cumsum_lanes(x, *, rel_end):
    rows, n = x.shape
    return pl.pallas_call(
        functools.partial(_cumsum_kernel, n_tiles=n // LANES, rel_end=rel_end),
        out_shape=jax.ShapeDtypeStruct((rows, n), F32),
        name="cumsum",
    )(x)


def _gla_kernel(q_ref, k_ref, v_ref, r_ref, gl_ref, gn_ref, s0_ref, o_ref, sf_ref, s_scr,
                *, n_invalid):
    t = pl.program_id(1)

    @pl.when(t == 0)
    def _():
        s_scr[...] = s0_ref[...]

    tl = 2 * GLA_CHUNK
    ri = lax.broadcasted_iota(jnp.int32, (tl, tl), 0)
    ci = lax.broadcasted_iota(jnp.int32, (tl, tl), 1)
    same_chunk = (ri // GLA_CHUNK) == (ci // GLA_CHUNK)
    tri = ((ci <= ri) & same_chunk)
    tri_b = tri.astype(BF16)
    first = ri < GLA_CHUNK
    lane = lax.broadcasted_iota(jnp.int32, (1, tl), 1)
    row1 = lax.broadcasted_iota(jnp.int32, (tl, 1), 0)
    scale = GLA_DK ** -0.5
    valid = (row1 + t * tl) >= n_invalid

    for p in range(GLA_HEADS // 2):
        cs = slice(p * tl, (p + 1) * tl)
        g = gl_ref[0, :, cs]
        hi, mid, lo = _split3(g)
        bcum = (_dot(tri_b, hi) + _dot(tri_b, mid)) + _dot(tri_b, lo)
        half = GLA_CHUNK // 2
        bref = jnp.where(first, bcum[half - 1:half, :], bcum[GLA_CHUNK + half - 1:GLA_CHUNK + half, :])
        blast = jnp.where(first, bcum[GLA_CHUNK - 1:GLA_CHUNK, :], bcum[tl - 1:tl, :])
        q = q_ref[0, :, cs]
        k = jnp.where(valid, k_ref[0, :, cs], 0.0)
        q_in = q * jnp.exp(bcum - bref)
        k_in = (k * jnp.exp(bref - bcum)).astype(BF16)
        qe = q * jnp.exp(bcum)
        kdt = (k * jnp.exp(blast - bcum)).T
        bcum_t = bcum.T
        dec0 = jnp.exp(bcum_t[:, GLA_CHUNK - 1:GLA_CHUNK])
        dec1 = jnp.exp(bcum_t[:, tl - 1:tl])
        kdt0 = jnp.where(lane < GLA_CHUNK, kdt, 0.0).astype(BF16)
        kdt1 = jnp.where(lane >= GLA_CHUNK, kdt, 0.0).astype(BF16)

        v_h = [v_ref[0, :, (2 * p + hh) * GLA_DV:(2 * p + hh + 1) * GLA_DV] for hh in range(2)]
        s_in = s_scr[p]
        ds0 = jnp.where(first, _dot(kdt0, v_h[0]), _dot(kdt0, v_h[1]))
        ds1 = jnp.where(first, _dot(kdt1, v_h[0]), _dot(kdt1, v_h[1]))
        s_mid = dec0 * s_in + ds0
        s_out = dec1 * s_mid + ds1
        s_scr[p] = s_out
        sf_ref[0, p] = s_out
        s_in_b = s_in.astype(BF16)
        s_mid_b = s_mid.astype(BF16)

        for hh in range(2):
            h = 2 * p + hh
            mh = (lane // GLA_DK) == hh
            a = _dot_nt(jnp.where(mh, q_in, 0.0).astype(BF16), k_in) * scale
            a = jnp.where(tri, a, 0.0)
            o_intra = _dot(a.astype(BF16), v_h[hh])
            qe_m = jnp.where(mh, qe, 0.0).astype(BF16)
            o_inter = jnp.where(row1 < GLA_CHUNK, _dot(qe_m, s_in_b), _dot(qe_m, s_mid_b)) * scale
            o = o_intra + o_inter
            hs = slice(h * GLA_DV, (h + 1) * GLA_DV)
            rn = o * lax.rsqrt(jnp.mean(o * o, axis=-1, keepdims=True) + LN_EPS) * gn_ref[:, hs]
            ra = r_ref[0, :, hs]
            o_ref[0, :, hs] = (rn * (ra * jax.nn.sigmoid(ra))).astype(BF16)


def _gla(gq, gk, gv, gr, glog, gn, s0, *, n_invalid):
    bsz, s, _ = gq.shape
    tl = 2 * GLA_CHUNK
    grid = (bsz, s // tl)
    tok = lambda w: pl.BlockSpec((1, tl, w), lambda b, t: (b, t, 0))
    return pl.pallas_call(
        functools.partial(_gla_kernel, n_invalid=n_invalid),
        out_shape=(jax.ShapeDtypeStruct((bsz, s, GLA_V), BF16),
                   jax.ShapeDtypeStruct((bsz, 2, tl, GLA_DV), F32)),
        grid=grid,
        in_specs=[tok(GLA_QK), tok(GLA_QK), tok(GLA_V), tok(GLA_V), tok(GLA_QK),
                  pl.BlockSpec((1, GLA_V), lambda b, t: (0, 0)),
                  pl.BlockSpec((2, tl, GLA_DV), lambda b, t: (0, 0, 0))],
        out_specs=(tok(GLA_V), pl.BlockSpec((1, 2, tl, GLA_DV), lambda b, t: (b, 0, 0, 0))),
        scratch_shapes=[pltpu.VMEM((2, tl, GLA_DV), F32)],
        compiler_params=pltpu.CompilerParams(dimension_semantics=("parallel", "arbitrary")),
        name="gla",
    )(gq, gk, gv, gr, glog, gn, s0)


def _fox_kernel(qt_ref, k_ref, vt_ref, cx_ref, kp_ref, vpt_ref, cm_ref, g_ref, o_ref,
                cb_scr, cbm_scr, m_scr, l_scr, acc_scr, *, tq):
    hp = pl.program_id(1)
    qi = pl.program_id(2)
    n_rep = tq // LANES

    @pl.when(qi == 0)
    def _():
        rows = lax.broadcasted_iota(jnp.int32, (16, LANES), 0)
        for hh in range(2):
            pick = rows == 2 * hp + hh

            def column_tile(chunk):
                crow = jnp.sum(jnp.where(pick, chunk, 0.0), axis=0, keepdims=True)
                return jnp.broadcast_to(crow, (LANES, LANES)).T

            cbm_scr[hh] = column_tile(cm_ref[...])
            for j in range(cx_ref.shape[2] // LANES):
                sl = slice(j * LANES, (j + 1) * LANES)
                cb_scr[hh, sl, :] = column_tile(cx_ref[0, :, sl])

    qt = qt_ref[0]
    lane = lax.broadcasted_iota(jnp.int32, (1, LANES), 1)
    qstart = pl.multiple_of(qi * tq, tq)
    for hh in range(2):
        m_scr[hh] = jnp.full((1, tq), -jnp.inf, F32)
        l_scr[hh] = jnp.zeros((1, tq), F32)
        acc_scr[hh] = jnp.zeros((LANES, tq), F32)
    cref = [cb_scr[hh, pl.ds(qstart, 1), :] for hh in range(2)]

    def update(kblk, vtblk, slabs, mask):
        for hh in range(2):
            km = jnp.where((lane // FOX_DH) == hh, kblk, jnp.zeros_like(kblk))
            bias = cref[hh] - slabs[hh]
            s = _dot(km, qt) + jnp.concatenate([bias] * n_rep, axis=1)
            if mask is not None:
                s = jnp.where(mask, s, NEG)
            m_prev = m_scr[hh]
            m_new = jnp.maximum(m_prev, jnp.max(s, axis=0, keepdims=True))
            alpha = jnp.exp(m_prev - m_new)
            p = jnp.exp(s - m_new)
            l_scr[hh] = alpha * l_scr[hh] + jnp.sum(p, axis=0, keepdims=True)
            acc_scr[hh] = alpha * acc_scr[hh] + _dot(vtblk, p.astype(BF16))
            m_scr[hh] = m_new

    key_row = lax.broadcasted_iota(jnp.int32, (PREFIX, tq), 0)
    update(kp_ref[0], vpt_ref[0], [cbm_scr[0], cbm_scr[1]], key_row >= N_PAD)

    def body(j, carry):
        ks = pl.multiple_of(j * tq, tq)
        update(k_ref[0, pl.ds(ks, tq), :], vt_ref[0, :, pl.ds(ks, tq)],
               [cb_scr[0, pl.ds(ks, tq), :], cb_scr[1, pl.ds(ks, tq), :]], None)
        return carry

    lax.fori_loop(0, qi, body, 0)

    ri = lax.broadcasted_iota(jnp.int32, (tq, tq), 0)
    ci = lax.broadcasted_iota(jnp.int32, (tq, tq), 1)
    update(k_ref[0, pl.ds(qstart, tq), :], vt_ref[0, :, pl.ds(qstart, tq)],
           [cb_scr[0, pl.ds(qstart, tq), :], cb_scr[1, pl.ds(qstart, tq), :]], ri <= ci)

    first = lax.broadcasted_iota(jnp.int32, (LANES, 1), 0) < FOX_DH
    ot = jnp.where(first, acc_scr[0] / l_scr[0], acc_scr[1] / l_scr[1])
    sq = ot * ot
    ms0 = jnp.sum(jnp.where(first, sq, 0.0), axis=0, keepdims=True) / FOX_DH
    ms1 = jnp.sum(jnp.where(first, 0.0, sq), axis=0, keepdims=True) / FOX_DH
    rs = jnp.where(first, lax.rsqrt(ms0 + LN_EPS), lax.rsqrt(ms1 + LN_EPS))
    o_ref[0] = (ot * rs * g_ref[...]).T.astype(BF16)


def _fox(fqt, fk, fvt, cx, fk_pre, fvt_pre, cmeta, gcol, *, tq):
    bsz, s, _ = fk.shape
    grid = (bsz, FOX_HEADS // 2, s // tq)
    return pl.pallas_call(
        functools.partial(_fox_kernel, tq=tq),
        out_shape=jax.ShapeDtypeStruct((bsz, s, FOX_W), BF16),
        grid=grid,
        in_specs=[pl.BlockSpec((1, LANES, tq), lambda b, hp, i: (b, hp, i)),
                  pl.BlockSpec((1, s, LANES), lambda b, hp, i: (b, 0, hp)),
                  pl.BlockSpec((1, LANES, s), lambda b, hp, i: (b, hp, 0)),
                  pl.BlockSpec((1, 16, s), lambda b, hp, i: (b, 0, 0)),
                  pl.BlockSpec((1, PREFIX, LANES), lambda b, hp, i: (0, 0, hp)),
                  pl.BlockSpec((1, LANES, PREFIX), lambda b, hp, i: (0, hp, 0)),
                  pl.BlockSpec((16, PREFIX), lambda b, hp, i: (0, 0)),
                  pl.BlockSpec((LANES, 1), lambda b, hp, i: (hp, 0))],
        out_specs=pl.BlockSpec((1, tq, LANES), lambda b, hp, i: (b, i, hp)),
        scratch_shapes=[pltpu.VMEM((2, s, LANES), F32), pltpu.VMEM((2, PREFIX, LANES), F32),
                        pltpu.VMEM((2, 1, tq), F32), pltpu.VMEM((2, 1, tq), F32),
                        pltpu.VMEM((2, LANES, tq), F32)],
        compiler_params=pltpu.CompilerParams(
            dimension_semantics=("parallel", "parallel", "arbitrary")),
        name="fox",
    )(fqt, fk, fvt, cx, fk_pre, fvt_pre, cmeta, gcol)


def _out_proj_kernel(oa_ref, ob_ref, x_ref, eg_ref, eb_ref, w_ref, g_ref, b_ref, h1_ref):
    o = jnp.concatenate([oa_ref[0], ob_ref[0]], axis=-1)
    mix = _dot(o, w_ref[...])
    h0 = _layer_norm(x_ref[0], eg_ref[...], eb_ref[...])
    h1_ref[0] = _layer_norm(DN_ALPHA * h0 + mix, g_ref[...], b_ref[...])


def _out_proj(oa, ob, x, eg, eb, w, g, b, *, tm):
    bsz, s, d = x.shape
    grid = (bsz, s // tm)
    tok = lambda w_: pl.BlockSpec((1, tm, w_), lambda bb, i: (bb, i, 0))
    full = lambda a: pl.BlockSpec(a.shape, lambda bb, i: (0,) * a.ndim)
    return pl.pallas_call(
        _out_proj_kernel,
        out_shape=jax.ShapeDtypeStruct((bsz, s, d), F32),
        grid=grid,
        in_specs=[tok(GLA_V), tok(FOX_W), tok(d), full(eg), full(eb), full(w), full(g), full(b)],
        out_specs=tok(d),
        compiler_params=pltpu.CompilerParams(
            dimension_semantics=("parallel", "parallel"), vmem_limit_bytes=VMEM_LIMIT),
        name="out_proj",
    )(oa, ob, x, eg, eb, w, g, b)


def _extract_top(curs, iota, n_pick, idx_big):
    curs = list(curs)
    vals = [[] for _ in curs]
    idxs = [[] for _ in curs]
    for _ in range(n_pick):
        for c, cur in enumerate(curs):
            m = jnp.max(cur, axis=0, keepdims=True)
            idx = jnp.min(jnp.where(cur == m, iota, idx_big), axis=0, keepdims=True)
            vals[c].append(m)
            idxs[c].append(idx)
            curs[c] = jnp.where(iota == idx, -jnp.inf, cur)
    return vals, idxs, curs


_HEADS_PER_STEP = 4


def _route_kernel(h_ref, wq_ref, keys_ref, lim_ref, e0_ref, rj_ref, e1_ref,
                  q_scr, cand_scr, sel_scr, *, tm):
    q_scr[...] = _dot(h_ref[...].astype(BF16), wq_ref[...]).astype(BF16)
    cand_scr[...] = jnp.full(cand_scr.shape, -jnp.inf, F32)
    iota_n = lax.broadcasted_iota(jnp.int32, (PEER_NKEYS, LANES), 0)
    iota_c = lax.broadcasted_iota(jnp.int32, (_N_CAND, LANES), 0)
    row_start = [next(r for r, (a, _) in enumerate(_PAIRS) if a == aa) for aa in range(PEER_TOPK)]
    row_end = row_start[1:] + [len(_PAIRS)]
    hs = _HEADS_PER_STEP

    def per_tile(u, carry_u):
        ts = pl.multiple_of(u * LANES, LANES)

        def per_group(hg, carry_h):
            scores = []
            for k in range(2 * hs):
                hc = 2 * hs * hg + k
                col = pl.multiple_of(hc * PEER_HALF, PEER_HALF)
                qhc = q_scr[pl.ds(ts, LANES), pl.ds(col, PEER_HALF)]
                scores.append(_dot_nt(keys_ref[hc], qhc))
            vals, idxs, _ = _extract_top(scores, iota_n, PEER_TOPK, PEER_NKEYS)
            for j in range(hs):
                tv0, tv1 = vals[2 * j], vals[2 * j + 1]
                for r, (a, b) in enumerate(_PAIRS):
                    cand_scr[j, r:r + 1, :] = tv0[a] + tv1[b]
            cands = [cand_scr[j] for j in range(hs)]
            _, _, rest = _extract_top(cands, iota_c, PEER_TOPK, _N_CAND)
            for j in range(hs):
                h = hs * hg + j
                s0, s1 = scores[2 * j], scores[2 * j + 1]
                picked = jnp.where((rest[j] == -jnp.inf) & (iota_c < len(_PAIRS)), 1.0, 0.0)
                z = jnp.sum(picked * jnp.exp(cands[j] - cands[j][0:1, :]), axis=0, keepdims=True)
                sel_scr[j] = picked
                lim = jnp.zeros((PEER_NKEYS, LANES), F32)
                rj = jnp.full((PEER_NKEYS, LANES), float(PEER_TOPK), F32)
                for a in range(PEER_TOPK):
                    nb = jnp.sum(sel_scr[j, row_start[a]:row_end[a], :], axis=0, keepdims=True)
                    lim = jnp.where(iota_n == idxs[2 * j][a], nb, lim)
                    rj = jnp.where(iota_n == idxs[2 * j + 1][a], float(a), rj)
                lim_ref[h, :, pl.ds(ts, LANES)] = lim
                e0_ref[h, :, pl.ds(ts, LANES)] = jnp.exp(s0 - vals[2 * j][0]) * (1.0 / z)
                rj_ref[h, :, pl.ds(ts, LANES)] = _to_words(rj)
                e1_ref[h, :, pl.ds(ts, LANES)] = _to_words(jnp.exp(s1 - vals[2 * j + 1][0]))
            return carry_h

        lax.fori_loop(0, PEER_HEADS // hs, per_group, 0)
        return carry_u

    lax.fori_loop(0, tm // LANES, per_tile, 0)


def _route(h1, wq, keys, *, tm):
    t, d = h1.shape
    side = jax.ShapeDtypeStruct((PEER_HEADS, PEER_NKEYS, t), F32)
    nw = PEER_NKEYS // _pack_factor()
    side_w = jax.ShapeDtypeStruct((PEER_HEADS, nw, t), jnp.uint32)
    side_spec = pl.BlockSpec((PEER_HEADS, PEER_NKEYS, tm), lambda i: (0, 0, i))
    side_w_spec = pl.BlockSpec((PEER_HEADS, nw, tm), lambda i: (0, 0, i))
    return pl.pallas_call(
        functools.partial(_route_kernel, tm=tm),
        out_shape=(side, side, side_w, side_w),
        grid=(t // tm,),
        in_specs=[pl.BlockSpec((tm, d), lambda i: (i, 0)),
                  pl.BlockSpec(wq.shape, lambda i: (0, 0)),
                  pl.BlockSpec(keys.shape, lambda i: (0, 0, 0))],
        out_specs=(side_spec, side_spec, side_w_spec, side_w_spec),
        scratch_shapes=[pltpu.VMEM((tm, 2 * PEER_HEADS * PEER_HALF), BF16),
                        pltpu.VMEM((_HEADS_PER_STEP, _N_CAND, LANES), F32),
                        pltpu.VMEM((_HEADS_PER_STEP, _N_CAND, LANES), F32)],
        compiler_params=pltpu.CompilerParams(
            dimension_semantics=("parallel",), vmem_limit_bytes=VMEM_LIMIT),
        name="route",
    )(h1, wq, keys)


SUBLANES = 8
_JB = 2
_IB = 4


def _pack_factor():
    return 4 // jnp.dtype(BF16).itemsize


def _to_words(x):
    return pltpu.bitcast(x.astype(BF16), jnp.uint32)


def _from_words(w):
    return pltpu.bitcast(w, BF16)


def _peer_kernel(h_ref, u_ref, vt_ref, lim_ref, e0_ref, rj_ref, e1_ref, g_ref, b_ref, o_ref,
                 hb_scr, act_scr, limb_scr, e0b_scr, p_scr, acc_scr, *, ni, tm, sub):
    ib = pl.program_id(1)
    pk = _pack_factor()
    rows = SUBLANES * pk
    prev, new = 0, 1

    @pl.when(ib == 0)
    def _():
        hb_scr[...] = h_ref[...].astype(BF16)
        acc_scr[...] = jnp.zeros(acc_scr.shape, F32)
        act_scr[prev] = jnp.zeros(act_scr.shape[1:], jnp.uint32)

    for h in range(PEER_HEADS):
        for ii in range(ni):
            limb_scr[h, ii] = _to_words(jnp.broadcast_to(lim_ref[h, ii:ii + 1, :], (rows, tm)))
            e0b_scr[h, ii] = _to_words(jnp.broadcast_to(e0_ref[h, ii:ii + 1, :], (rows, tm)))

    nst = tm // sub

    def activations(st):
        cs = slice(st * sub, (st + 1) * sub)
        at = _dot_nt(u_ref[...], hb_scr[cs, :])
        act_scr[new, st] = _to_words(0.5 * at * (1.0 + lax.erf(at * (2.0 ** -0.5))))

    def gates(st):
        cs = slice(st * sub, (st + 1) * sub)
        for lt in range(st * (sub // LANES), (st + 1) * (sub // LANES)):
            ls = slice(lt * LANES, (lt + 1) * LANES)
            for jb0 in range(0, PEER_NKEYS // rows, _JB):
                for ii0 in range(0, ni, _IB):
                    gate = [[jnp.zeros((rows, LANES), BF16) for _ in range(_IB)] for _ in range(_JB)]
                    for h in range(PEER_HEADS):
                        js = [slice((jb0 + a) * SUBLANES, (jb0 + a + 1) * SUBLANES) for a in range(_JB)]
                        rjv = [_from_words(rj_ref[h, js[a], ls]) for a in range(_JB)]
                        e1v = [_from_words(e1_ref[h, js[a], ls]) for a in range(_JB)]
                        for b in range(_IB):
                            lm = _from_words(limb_scr[h, ii0 + b, :, ls])
                            ev = _from_words(e0b_scr[h, ii0 + b, :, ls])
                            for a in range(_JB):
                                prod = e1v[a] * ev
                                gate[a][b] = gate[a][b] + jnp.where(rjv[a] < lm, prod, jnp.zeros_like(prod))
                    for a in range(_JB):
                        for b in range(_IB):
                            w0 = (ii0 + b) * (PEER_NKEYS // pk) + (jb0 + a) * SUBLANES
                            ws = slice(w0, w0 + SUBLANES)
                            lo = slice(ls.start - cs.start, ls.stop - cs.start)
                            p_scr[st, ws, lo] = _to_words(gate[a][b] * _from_words(act_scr[prev, st, ws, lo]))

    def accumulate(st):
        acc_scr[st] += _dot(vt_ref[...], _from_words(p_scr[st]))

    for st in range(nst):
        activations(st)
        if st > 0:
            accumulate(st - 1)
        gates(st)
    accumulate(nst - 1)
    act_scr[prev] = act_scr[new]

    @pl.when(ib == pl.num_programs(1) - 1)
    def _():
        for st in range(tm // sub):
            cs = slice(st * sub, (st + 1) * sub)
            y = acc_scr[st].T
            o_ref[cs, :] = _layer_norm(DN_ALPHA * h_ref[cs, :] + y, g_ref[...], b_ref[...])


def _peer(h1, u, vt, lim, e0, rj, e1, g, b, *, tm, ni):
    t, d = h1.shape
    n_exp = u.shape[0]
    te = ni * PEER_NKEYS
    n_tiles = n_exp // te
    grid = (t // tm, n_tiles + 1)
    pk = _pack_factor()
    words = lambda shape: pltpu.VMEM(shape, jnp.uint32)
    sub = min(tm, 2 * LANES)
    nst = tm // sub
    act_tile = lambda j: jnp.minimum(j, n_tiles - 1)
    fin_tile = lambda j: jnp.maximum(j - 1, 0)
    return pl.pallas_call(
        functools.partial(_peer_kernel, ni=ni, tm=tm, sub=sub),
        out_shape=jax.ShapeDtypeStruct((t, d), F32),
        grid=grid,
        in_specs=[pl.BlockSpec((tm, d), lambda i, j: (i, 0), pipeline_mode=pl.Buffered(1)),
                  pl.BlockSpec((te, d), lambda i, j: (act_tile(j), 0)),
                  pl.BlockSpec((d, te), lambda i, j: (0, fin_tile(j))),
                  pl.BlockSpec((PEER_HEADS, ni, tm), lambda i, j: (0, fin_tile(j), i)),
                  pl.BlockSpec((PEER_HEADS, ni, tm), lambda i, j: (0, fin_tile(j), i)),
                  pl.BlockSpec((PEER_HEADS, PEER_NKEYS // pk, tm), lambda i, j: (0, 0, i),
                               pipeline_mode=pl.Buffered(1)),
                  pl.BlockSpec((PEER_HEADS, PEER_NKEYS // pk, tm), lambda i, j: (0, 0, i),
                               pipeline_mode=pl.Buffered(1)),
                  pl.BlockSpec(g.shape, lambda i, j: (0, 0)),
                  pl.BlockSpec(b.shape, lambda i, j: (0, 0))],
        out_specs=pl.BlockSpec((tm, d), lambda i, j: (i, 0)),
        scratch_shapes=[pltpu.VMEM((tm, d), BF16), words((2, nst, te // pk, sub)),
                        words((PEER_HEADS, ni, SUBLANES, tm)),
                        words((PEER_HEADS, ni, SUBLANES, tm)),
                        words((nst, te // pk, sub)), pltpu.VMEM((nst, d, sub), F32)],
        compiler_params=pltpu.CompilerParams(
            dimension_semantics=("parallel", "arbitrary"), vmem_limit_bytes=VMEM_LIMIT),
        name="peer",
    )(h1, u, vt, lim, e0, rj, e1, g, b)


def _pick_tile(n, pref):
    t = pref
    while n % t:
        t //= 2
    return t


def kernel(x, meta_tokens, emb_ln_g, emb_ln_b, w_in, w_gate_up, b_gate, b_forget, gla_norm_g,
           fox_norm_g, w_out, ln1_g, ln1_b, peer_w_q, peer_sub_keys, peer_u, peer_v, ln2_g, ln2_b):
    bsz, s, d = x.shape
    assert d == D_MODEL and s % (2 * GLA_CHUNK) == 0 and w_in.shape[0] == 1
    row = lambda v: v.reshape(1, -1).astype(F32)

    w0 = w_in[0]
    o_ga = 2 * GLA_QK + 2 * GLA_V
    o_fb = o_ga + GLA_RANK + 3 * FOX_W
    o_qb = o_ga + GLA_RANK
    o_kb, o_vb = o_qb + FOX_W, o_qb + 2 * FOX_W
    wm = jnp.concatenate([w0[:, :o_ga], w0[:, o_kb:o_vb]], axis=1).astype(BF16)
    wqv = jnp.concatenate([w0[:, o_qb:o_kb], w0[:, o_vb:o_fb]], axis=1).T.astype(BF16)
    wga =jnp.pad(w0[:, o_ga:o_ga + GLA_RANK], ((0, 0), (0, LANES - GLA_RANK))).astype(BF16)
    wft = jnp.pad(w0[:, o_fb:].T, ((0, 16 - FOX_HEADS), (0, 0))).astype(BF16)
    wup = jnp.pad(w_gate_up[0], ((0, LANES - GLA_RANK), (0, 0))).astype(BF16)
    bg = row(b_gate[0])
    bf = jnp.pad(b_forget[0].astype(F32), (0, 16 - FOX_HEADS)).reshape(16, 1)
    eg, eb = row(emb_ln_g), row(emb_ln_b)

    x_pre = jnp.concatenate([jnp.zeros((N_PAD, d), x.dtype), meta_tokens.astype(x.dtype)], axis=0)[None]
    pre = _in_proj(x_pre, eg, eb, wm, wqv, wga, wft, wup, bg, bf, tm=PREFIX)
    main = _in_proj(x, eg, eb, wm, wqv, wga, wft, wup, bg, bf, tm=_pick_tile(s, 256))
    gq_p, gk_p, gv_p, gr_p, gl_p, _, fk_p, fvt_p, lf_p = pre
    gq, gk, gv, gr, gl, fqt, fk, fvt, lf = main

    gn = row(gla_norm_g[0])
    s_zero = jnp.zeros((2, 2 * GLA_CHUNK, GLA_DV), F32)
    _, s_pre = _gla(gq_p, gk_p, gv_p, gr_p, gl_p, gn, s_zero, n_invalid=N_PAD)
    oa, _ = _gla(gq, gk, gv, gr, gl, gn, s_pre[0], n_invalid=0)

    cmeta = _cumsum_lanes(lf_p[0], rel_end=True)
    cx = _cumsum_lanes(lf.reshape(bsz * 16, s), rel_end=False).reshape(bsz, 16, s)
    ob = _fox(fqt, fk, fvt, cx, fk_p, fvt_p, cmeta, fox_norm_g[0].reshape(FOX_W, 1).astype(F32),
              tq=_pick_tile(s, 512))

    h1 = _out_proj(oa, ob, x, eg, eb, w_out[0].astype(BF16), row(ln1_g[0]), row(ln1_b[0]),
                   tm=_pick_tile(s, 512))

    t = bsz * s
    h1f = h1.reshape(t, d)
    keys = peer_sub_keys[0].reshape(2 * PEER_HEADS, PEER_NKEYS, PEER_HALF).astype(BF16)
    lim, e0, rj, e1 = _route(h1f, peer_w_q[0].astype(BF16), keys, tm=_pick_tile(t, 512))
    out = _peer(h1f, peer_u[0].astype(BF16), peer_v[0].astype(BF16).T, lim, e0, rj, e1,
                row(ln2_g[0]), row(ln2_b[0]), tm=_pick_tile(t, 1024), ni=8)
    return out.reshape(bsz, s, d)
```

```python
import functools

import jax
import jax.numpy as jnp
from jax import lax
from jax.experimental import pallas as pl
from jax.experimental.pallas import tpu as pltpu

F32 = jnp.float32
BF16 = jnp.bfloat16

D_MODEL = 1024
N_META = 16
PREFIX = 128
N_PAD = PREFIX - N_META

GLA_HEADS = 4
GLA_DK = 64
GLA_DV = 128
GLA_RANK = 16
GLA_TAU = 16.0
GLA_CHUNK = 64
GLA_QK = GLA_HEADS * GLA_DK
GLA_V = GLA_HEADS * GLA_DV

FOX_HEADS = 8
FOX_DH = 64
FOX_W = FOX_HEADS * FOX_DH

PEER_HEADS = 8
PEER_NKEYS = 128
PEER_TOPK = 16
PEER_HALF = 128

DN_ALPHA = 2.0 ** 0.25
LN_EPS = 1e-5
NEG = -1e30

LANES = 128
VMEM_LIMIT = 48 * 1024 * 1024

_PAIRS = tuple((a, b) for a in range(PEER_TOPK) for b in range(PEER_TOPK)
               if (a + 1) * (b + 1) <= PEER_TOPK)
_N_CAND = 56


def _dot(a, b):
    return jnp.dot(a, b, preferred_element_type=F32)


def _dot_nt(a, b):
    return lax.dot_general(a, b, (((1,), (1,)), ((), ())), preferred_element_type=F32)


def _layer_norm(x, g, b):
    mu = jnp.mean(x, axis=-1, keepdims=True)
    xc = x - mu
    var = jnp.mean(xc * xc, axis=-1, keepdims=True)
    return xc * lax.rsqrt(var + LN_EPS) * g + b


def _split3(x):
    hi = x.astype(BF16)
    r1 = x - hi.astype(F32)
    mid = r1.astype(BF16)
    lo = (r1 - mid.astype(F32)).astype(BF16)
    return hi, mid, lo


def _in_proj_kernel(x_ref, g_ref, b_ref, wm_ref, wqv_ref, wga_ref, wft_ref, wup_ref, bg_ref, bf_ref,
                    gq_ref, gk_ref, gv_ref, gr_ref, glog_ref, fqt_ref, fk_ref, fvt_ref, lf_ref):
    hn = _layer_norm(x_ref[0], g_ref[...], b_ref[...])
    hb = hn.astype(BF16)
    proj = _dot(hb, wm_ref[...])
    gq_ref[0] = proj[:, 0:256]
    gk_ref[0] = proj[:, 256:512]
    gv_ref[0] = proj[:, 512:1024].astype(BF16)
    gr_ref[0] = proj[:, 1024:1536]
    fk_ref[0] = proj[:, 1536:2048].astype(BF16)
    qvt = _dot_nt(wqv_ref[...], hb)
    fqt_ref[0] = (qvt[0:FOX_W, :] * (FOX_DH ** -0.5)).astype(BF16)
    fvt_ref[0] = qvt[FOX_W:2 * FOX_W, :].astype(BF16)
    ga = _dot(hb, wga_ref[...])
    z = _dot(ga.astype(BF16), wup_ref[...]) + bg_ref[...]
    glog_ref[0] = jax.nn.log_sigmoid(z) / GLA_TAU
    fbt = _dot_nt(wft_ref[...], hb)
    lf_ref[0] = jax.nn.log_sigmoid(fbt + bf_ref[...])


def _in_proj(x, ln_g, ln_b, wm, wqv, wga, wft, wup, bg, bf, *, tm):
    bsz, s, d = x.shape
    grid = (bsz, s // tm)
    tok = lambda w: pl.BlockSpec((1, tm, w), lambda b, i: (b, i, 0))
    feat = lambda r: pl.BlockSpec((1, r, tm), lambda b, i: (b, 0, i))
    full = lambda a: pl.BlockSpec(a.shape, lambda b, i: (0,) * a.ndim)
    out_shape = (
        jax.ShapeDtypeStruct((bsz, s, GLA_QK), F32),
        jax.ShapeDtypeStruct((bsz, s, GLA_QK), F32),
        jax.ShapeDtypeStruct((bsz, s, GLA_V), BF16),
        jax.ShapeDtypeStruct((bsz, s, GLA_V), F32),
        jax.ShapeDtypeStruct((bsz, s, GLA_QK), F32),
        jax.ShapeDtypeStruct((bsz, FOX_W, s), BF16),
        jax.ShapeDtypeStruct((bsz, s, FOX_W), BF16),
        jax.ShapeDtypeStruct((bsz, FOX_W, s), BF16),
        jax.ShapeDtypeStruct((bsz, 16, s), F32),
    )
    out_specs = (tok(GLA_QK), tok(GLA_QK), tok(GLA_V), tok(GLA_V), tok(GLA_QK),
                 feat(FOX_W), tok(FOX_W), feat(FOX_W), feat(16))
    return pl.pallas_call(
        _in_proj_kernel,
        out_shape=out_shape,
        grid=grid,
        in_specs=[tok(d), full(ln_g), full(ln_b), full(wm), full(wqv), full(wga), full(wft),
                  full(wup), full(bg), full(bf)],
        out_specs=out_specs,
        compiler_params=pltpu.CompilerParams(
            dimension_semantics=("parallel", "parallel"), vmem_limit_bytes=VMEM_LIMIT),
        name="in_proj",
    )(x, ln_g, ln_b, wm, wqv, wga, wft, wup, bg, bf)


def _cumsum_kernel(x_ref, o_ref, *, n_tiles, rel_end):
    ri = lax.broadcasted_iota(jnp.int32, (LANES, LANES), 0)
    ci = lax.broadcasted_iota(jnp.int32, (LANES, LANES), 1)
    upper = (ri <= ci).astype(BF16)
    carry = jnp.zeros((x_ref.shape[0], 1), F32)
    for j in range(n_tiles):
        sl = slice(j * LANES, (j + 1) * LANES)
        hi, mid, lo = _split3(x_ref[:, sl])
        cs = (_dot(hi, upper) + _dot(mid, upper)) + _dot(lo, upper) + carry
        o_ref[:, sl] = cs
        carry = cs[:, LANES - 1:LANES]
    if rel_end:
        o_ref[...] = o_ref[...] - carry


def _cumsum_lanes(x, *, rel_end):
    rows, n = x.shape
    return pl.pallas_call(
        functools.partial(_cumsum_kernel, n_tiles=n // LANES, rel_end=rel_end),
        out_shape=jax.ShapeDtypeStruct((rows, n), F32),
        name="cumsum",
    )(x)


def _gla_kernel(q_ref, k_ref, v_ref, r_ref, gl_ref, gn_ref, s0_ref, o_ref, sf_ref, s_scr,
                *, n_invalid):
    t = pl.program_id(1)

    @pl.when(t == 0)
    def _():
        s_scr[...] = s0_ref[...]

    tl = 2 * GLA_CHUNK
    ri = lax.broadcasted_iota(jnp.int32, (tl, tl), 0)
    ci = lax.broadcasted_iota(jnp.int32, (tl, tl), 1)
    same_chunk = (ri // GLA_CHUNK) == (ci // GLA_CHUNK)
    tri = ((ci <= ri) & same_chunk)
    tri_b = tri.astype(BF16)
    first = ri < GLA_CHUNK
    lane = lax.broadcasted_iota(jnp.int32, (1, tl), 1)
    row1 = lax.broadcasted_iota(jnp.int32, (tl, 1), 0)
    scale = GLA_DK ** -0.5
    valid = (row1 + t * tl) >= n_invalid

    for p in range(GLA_HEADS // 2):
        cs = slice(p * tl, (p + 1) * tl)
        g = gl_ref[0, :, cs]
        hi, mid, lo = _split3(g)
        bcum = (_dot(tri_b, hi) + _dot(tri_b, mid)) + _dot(tri_b, lo)
        half = GLA_CHUNK // 2
        bref = jnp.where(first, bcum[half - 1:half, :], bcum[GLA_CHUNK + half - 1:GLA_CHUNK + half, :])
        blast = jnp.where(first, bcum[GLA_CHUNK - 1:GLA_CHUNK, :], bcum[tl - 1:tl, :])
        q = q_ref[0, :, cs]
        k = jnp.where(valid, k_ref[0, :, cs], 0.0)
        q_in = q * jnp.exp(bcum - bref)
        k_in = (k * jnp.exp(bref - bcum)).astype(BF16)
        qe = q * jnp.exp(bcum)
        kdt = (k * jnp.exp(blast - bcum)).T
        bcum_t = bcum.T
        dec0 = jnp.exp(bcum_t[:, GLA_CHUNK - 1:GLA_CHUNK])
        dec1 = jnp.exp(bcum_t[:, tl - 1:tl])
        kdt0 = jnp.where(lane < GLA_CHUNK, kdt, 0.0).astype(BF16)
        kdt1 = jnp.where(lane >= GLA_CHUNK, kdt, 0.0).astype(BF16)

        v_h = [v_ref[0, :, (2 * p + hh) * GLA_DV:(2 * p + hh + 1) * GLA_DV] for hh in range(2)]
        s_in = s_scr[p]
        ds0 = jnp.where(first, _dot(kdt0, v_h[0]), _dot(kdt0, v_h[1]))
        ds1 = jnp.where(first, _dot(kdt1, v_h[0]), _dot(kdt1, v_h[1]))
        s_mid = dec0 * s_in + ds0
        s_out = dec1 * s_mid + ds1
        s_scr[p] = s_out
        sf_ref[0, p] = s_out
        s_in_b = s_in.astype(BF16)
        s_mid_b = s_mid.astype(BF16)

        for hh in range(2):
            h = 2 * p + hh
            mh = (lane // GLA_DK) == hh
            a = _dot_nt(jnp.where(mh, q_in, 0.0).astype(BF16), k_in) * scale
            a = jnp.where(tri, a, 0.0)
            o_intra = _dot(a.astype(BF16), v_h[hh])
            qe_m = jnp.where(mh, qe, 0.0).astype(BF16)
            o_inter = jnp.where(row1 < GLA_CHUNK, _dot(qe_m, s_in_b), _dot(qe_m, s_mid_b)) * scale
            o = o_intra + o_inter
            hs = slice(h * GLA_DV, (h + 1) * GLA_DV)
            rn = o * lax.rsqrt(jnp.mean(o * o, axis=-1, keepdims=True) + LN_EPS) * gn_ref[:, hs]
            ra = r_ref[0, :, hs]
            o_ref[0, :, hs] = (rn * (ra * jax.nn.sigmoid(ra))).astype(BF16)


def _gla(gq, gk, gv, gr, glog, gn, s0, *, n_invalid):
    bsz, s, _ = gq.shape
    tl = 2 * GLA_CHUNK
    grid = (bsz, s // tl)
    tok = lambda w: pl.BlockSpec((1, tl, w), lambda b, t: (b, t, 0))
    return pl.pallas_call(
        functools.partial(_gla_kernel, n_invalid=n_invalid),
        out_shape=(jax.ShapeDtypeStruct((bsz, s, GLA_V), BF16),
                   jax.ShapeDtypeStruct((bsz, 2, tl, GLA_DV), F32)),
        grid=grid,
        in_specs=[tok(GLA_QK), tok(GLA_QK), tok(GLA_V), tok(GLA_V), tok(GLA_QK),
                  pl.BlockSpec((1, GLA_V), lambda b, t: (0, 0)),
                  pl.BlockSpec((2, tl, GLA_DV), lambda b, t: (0, 0, 0))],
        out_specs=(tok(GLA_V), pl.BlockSpec((1, 2, tl, GLA_DV), lambda b, t: (b, 0, 0, 0))),
        scratch_shapes=[pltpu.VMEM((2, tl, GLA_DV), F32)],
        compiler_params=pltpu.CompilerParams(dimension_semantics=("parallel", "arbitrary")),
        name="gla",
    )(gq, gk, gv, gr, glog, gn, s0)


def _fox_kernel(qt_ref, k_ref, vt_ref, cx_ref, kp_ref, vpt_ref, cm_ref, g_ref, o_ref,
                cb_scr, cbm_scr, m_scr, l_scr, acc_scr, *, tq):
    hp = pl.program_id(1)
    qi = pl.program_id(2)
    n_rep = tq // LANES

    @pl.when(qi == 0)
    def _():
        rows = lax.broadcasted_iota(jnp.int32, (16, LANES), 0)
        for hh in range(2):
            pick = rows == 2 * hp + hh

            def column_tile(chunk):
                crow = jnp.sum(jnp.where(pick, chunk, 0.0), axis=0, keepdims=True)
                return jnp.broadcast_to(crow, (LANES, LANES)).T

            cbm_scr[hh] = column_tile(cm_ref[...])
            for j in range(cx_ref.shape[2] // LANES):
                sl = slice(j * LANES, (j + 1) * LANES)
                cb_scr[hh, sl, :] = column_tile(cx_ref[0, :, sl])

    qt = qt_ref[0]
    lane = lax.broadcasted_iota(jnp.int32, (1, LANES), 1)
    qstart = pl.multiple_of(qi * tq, tq)
    for hh in range(2):
        m_scr[hh] = jnp.full((1, tq), -jnp.inf, F32)
        l_scr[hh] = jnp.zeros((1, tq), F32)
        acc_scr[hh] = jnp.zeros((LANES, tq), F32)
    cref = [cb_scr[hh, pl.ds(qstart, 1), :] for hh in range(2)]

    def update(kblk, vtblk, slabs, mask):
        for hh in range(2):
            km = jnp.where((lane // FOX_DH) == hh, kblk, jnp.zeros_like(kblk))
            bias = cref[hh] - slabs[hh]
            s = _dot(km, qt) + jnp.concatenate([bias] * n_rep, axis=1)
            if mask is not None:
                s = jnp.where(mask, s, NEG)
            m_prev = m_scr[hh]
            m_new = jnp.maximum(m_prev, jnp.max(s, axis=0, keepdims=True))
            alpha = jnp.exp(m_prev - m_new)
            p = jnp.exp(s - m_new)
            l_scr[hh] = alpha * l_scr[hh] + jnp.sum(p, axis=0, keepdims=True)
            acc_scr[hh] = alpha * acc_scr[hh] + _dot(vtblk, p.astype(BF16))
            m_scr[hh] = m_new

    key_row = lax.broadcasted_iota(jnp.int32, (PREFIX, tq), 0)
    update(kp_ref[0], vpt_ref[0], [cbm_scr[0], cbm_scr[1]], key_row >= N_PAD)

    def body(j, carry):
        ks = pl.multiple_of(j * tq, tq)
        update(k_ref[0, pl.ds(ks, tq), :], vt_ref[0, :, pl.ds(ks, tq)],
               [cb_scr[0, pl.ds(ks, tq), :], cb_scr[1, pl.ds(ks, tq), :]], None)
        return carry

    lax.fori_loop(0, qi, body, 0)

    ri = lax.broadcasted_iota(jnp.int32, (tq, tq), 0)
    ci = lax.broadcasted_iota(jnp.int32, (tq, tq), 1)
    update(k_ref[0, pl.ds(qstart, tq), :], vt_ref[0, :, pl.ds(qstart, tq)],
           [cb_scr[0, pl.ds(qstart, tq), :], cb_scr[1, pl.ds(qstart, tq), :]], ri <= ci)

    first = lax.broadcasted_iota(jnp.int32, (LANES, 1), 0) < FOX_DH
    ot = jnp.where(first, acc_scr[0] / l_scr[0], acc_scr[1] / l_scr[1])
    sq = ot * ot
    ms0 = jnp.sum(jnp.where(first, sq, 0.0), axis=0, keepdims=True) / FOX_DH
    ms1 = jnp.sum(jnp.where(first, 0.0, sq), axis=0, keepdims=True) / FOX_DH
    rs = jnp.where(first, lax.rsqrt(ms0 + LN_EPS), lax.rsqrt(ms1 + LN_EPS))
    o_ref[0] = (ot * rs * g_ref[...]).T.astype(BF16)


def _fox(fqt, fk, fvt, cx, fk_pre, fvt_pre, cmeta, gcol, *, tq):
    bsz, s, _ = fk.shape
    grid = (bsz, FOX_HEADS // 2, s // tq)
    return pl.pallas_call(
        functools.partial(_fox_kernel, tq=tq),
        out_shape=jax.ShapeDtypeStruct((bsz, s, FOX_W), BF16),
        grid=grid,
        in_specs=[pl.BlockSpec((1, LANES, tq), lambda b, hp, i: (b, hp, i)),
                  pl.BlockSpec((1, s, LANES), lambda b, hp, i: (b, 0, hp)),
                  pl.BlockSpec((1, LANES, s), lambda b, hp, i: (b, hp, 0)),
                  pl.BlockSpec((1, 16, s), lambda b, hp, i: (b, 0, 0)),
                  pl.BlockSpec((1, PREFIX, LANES), lambda b, hp, i: (0, 0, hp)),
                  pl.BlockSpec((1, LANES, PREFIX), lambda b, hp, i: (0, hp, 0)),
                  pl.BlockSpec((16, PREFIX), lambda b, hp, i: (0, 0)),
                  pl.BlockSpec((LANES, 1), lambda b, hp, i: (hp, 0))],
        out_specs=pl.BlockSpec((1, tq, LANES), lambda b, hp, i: (b, i, hp)),
        scratch_shapes=[pltpu.VMEM((2, s, LANES), F32), pltpu.VMEM((2, PREFIX, LANES), F32),
                        pltpu.VMEM((2, 1, tq), F32), pltpu.VMEM((2, 1, tq), F32),
                        pltpu.VMEM((2, LANES, tq), F32)],
        compiler_params=pltpu.CompilerParams(
            dimension_semantics=("parallel", "parallel", "arbitrary")),
        name="fox",
    )(fqt, fk, fvt, cx, fk_pre, fvt_pre, cmeta, gcol)


def _out_proj_kernel(oa_ref, ob_ref, x_ref, eg_ref, eb_ref, w_ref, g_ref, b_ref, h1_ref):
    o = jnp.concatenate([oa_ref[0], ob_ref[0]], axis=-1)
    mix = _dot(o, w_ref[...])
    h0 = _layer_norm(x_ref[0], eg_ref[...], eb_ref[...])
    h1_ref[0] = _layer_norm(DN_ALPHA * h0 + mix, g_ref[...], b_ref[...])


def _out_proj(oa, ob, x, eg, eb, w, g, b, *, tm):
    bsz, s, d = x.shape
    grid = (bsz, s // tm)
    tok = lambda w_: pl.BlockSpec((1, tm, w_), lambda bb, i: (bb, i, 0))
    full = lambda a: pl.BlockSpec(a.shape, lambda bb, i: (0,) * a.ndim)
    return pl.pallas_call(
        _out_proj_kernel,
        out_shape=jax.ShapeDtypeStruct((bsz, s, d), F32),
        grid=grid,
        in_specs=[tok(GLA_V), tok(FOX_W), tok(d), full(eg), full(eb), full(w), full(g), full(b)],
        out_specs=tok(d),
        compiler_params=pltpu.CompilerParams(
            dimension_semantics=("parallel", "parallel"), vmem_limit_bytes=VMEM_LIMIT),
        name="out_proj",
    )(oa, ob, x, eg, eb, w, g, b)


def _extract_top(curs, iota, n_pick, idx_big):
    curs = list(curs)
    vals = [[] for _ in curs]
    idxs = [[] for _ in curs]
    for _ in range(n_pick):
        for c, cur in enumerate(curs):
            m = jnp.max(cur, axis=0, keepdims=True)
            idx = jnp.min(jnp.where(cur == m, iota, idx_big), axis=0, keepdims=True)
            vals[c].append(m)
            idxs[c].append(idx)
            curs[c] = jnp.where(iota == idx, -jnp.inf, cur)
    return vals, idxs, curs


_HEADS_PER_STEP = 4


def _route_kernel(h_ref, wq_ref, keys_ref, lim_ref, e0_ref, rj_ref, e1_ref,
                  q_scr, cand_scr, sel_scr, *, tm):
    q_scr[...] = _dot(h_ref[...].astype(BF16), wq_ref[...]).astype(BF16)
    cand_scr[...] = jnp.full(cand_scr.shape, -jnp.inf, F32)
    iota_n = lax.broadcasted_iota(jnp.int32, (PEER_NKEYS, LANES), 0)
    iota_c = lax.broadcasted_iota(jnp.int32, (_N_CAND, LANES), 0)
    row_start = [next(r for r, (a, _) in enumerate(_PAIRS) if a == aa) for aa in range(PEER_TOPK)]
    row_end = row_start[1:] + [len(_PAIRS)]
    hs = _HEADS_PER_STEP

    def per_tile(u, carry_u):
        ts = pl.multiple_of(u * LANES, LANES)

        def per_group(hg, carry_h):
            scores = []
            for k in range(2 * hs):
                hc = 2 * hs * hg + k
                col = pl.multiple_of(hc * PEER_HALF, PEER_HALF)
                qhc = q_scr[pl.ds(ts, LANES), pl.ds(col, PEER_HALF)]
                scores.append(_dot_nt(keys_ref[hc], qhc))
            vals, idxs, _ = _extract_top(scores, iota_n, PEER_TOPK, PEER_NKEYS)
            for j in range(hs):
                tv0, tv1 = vals[2 * j], vals[2 * j + 1]
                for r, (a, b) in enumerate(_PAIRS):
                    cand_scr[j, r:r + 1, :] = tv0[a] + tv1[b]
            cands = [cand_scr[j] for j in range(hs)]
            _, _, rest = _extract_top(cands, iota_c, PEER_TOPK, _N_CAND)
            for j in range(hs):
                h = hs * hg + j
                s0, s1 = scores[2 * j], scores[2 * j + 1]
                picked = jnp.where((rest[j] == -jnp.inf) & (iota_c < len(_PAIRS)), 1.0, 0.0)
                z = jnp.sum(picked * jnp.exp(cands[j] - cands[j][0:1, :]), axis=0, keepdims=True)
                sel_scr[j] = picked
                lim = jnp.zeros((PEER_NKEYS, LANES), F32)
                rj = jnp.full((PEER_NKEYS, LANES), float(PEER_TOPK), F32)
                for a in range(PEER_TOPK):
                    nb = jnp.sum(sel_scr[j, row_start[a]:row_end[a], :], axis=0, keepdims=True)
                    lim = jnp.where(iota_n == idxs[2 * j][a], nb, lim)
                    rj = jnp.where(iota_n == idxs[2 * j + 1][a], float(a), rj)
                lim_ref[h, :, pl.ds(ts, LANES)] = lim
                e0_ref[h, :, pl.ds(ts, LANES)] = jnp.exp(s0 - vals[2 * j][0]) * (1.0 / z)
                rj_ref[h, :, pl.ds(ts, LANES)] = _to_words(rj)
                e1_ref[h, :, pl.ds(ts, LANES)] = _to_words(jnp.exp(s1 - vals[2 * j + 1][0]))
            return carry_h

        lax.fori_loop(0, PEER_HEADS // hs, per_group, 0)
        return carry_u

    lax.fori_loop(0, tm // LANES, per_tile, 0)


def _route(h1, wq, keys, *, tm):
    t, d = h1.shape
    side = jax.ShapeDtypeStruct((PEER_HEADS, PEER_NKEYS, t), F32)
    nw = PEER_NKEYS // _pack_factor()
    side_w = jax.ShapeDtypeStruct((PEER_HEADS, nw, t), jnp.uint32)
    side_spec = pl.BlockSpec((PEER_HEADS, PEER_NKEYS, tm), lambda i: (0, 0, i))
    side_w_spec = pl.BlockSpec((PEER_HEADS, nw, tm), lambda i: (0, 0, i))
    return pl.pallas_call(
        functools.partial(_route_kernel, tm=tm),
        out_shape=(side, side, side_w, side_w),
        grid=(t // tm,),
        in_specs=[pl.BlockSpec((tm, d), lambda i: (i, 0)),
                  pl.BlockSpec(wq.shape, lambda i: (0, 0)),
                  pl.BlockSpec(keys.shape, lambda i: (0, 0, 0))],
        out_specs=(side_spec, side_spec, side_w_spec, side_w_spec),
        scratch_shapes=[pltpu.VMEM((tm, 2 * PEER_HEADS * PEER_HALF), BF16),
                        pltpu.VMEM((_HEADS_PER_STEP, _N_CAND, LANES), F32),
                        pltpu.VMEM((_HEADS_PER_STEP, _N_CAND, LANES), F32)],
        compiler_params=pltpu.CompilerParams(
            dimension_semantics=("parallel",), vmem_limit_bytes=VMEM_LIMIT),
        name="route",
    )(h1, wq, keys)


SUBLANES = 8
_JB = 2
_IB = 4


def _pack_factor():
    return 4 // jnp.dtype(BF16).itemsize


def _to_words(x):
    return pltpu.bitcast(x.astype(BF16), jnp.uint32)


def _from_words(w):
    return pltpu.bitcast(w, BF16)


def _peer_kernel(h_ref, u_ref, vt_ref, lim_ref, e0_ref, rj_ref, e1_ref, g_ref, b_ref, o_ref,
                 hb_scr, act_scr, limb_scr, e0b_scr, p_scr, acc_scr, *, ni, tm, sub):
    ib = pl.program_id(1)
    pk = _pack_factor()
    rows = SUBLANES * pk

    @pl.when(ib == 0)
    def _():
        hb_scr[...] = h_ref[...].astype(BF16)
        acc_scr[...] = jnp.zeros(acc_scr.shape, F32)

    for h in range(PEER_HEADS):
        for ii in range(ni):
            limb_scr[h, ii] = _to_words(jnp.broadcast_to(lim_ref[h, ii:ii + 1, :], (rows, tm)))
            e0b_scr[h, ii] = _to_words(jnp.broadcast_to(e0_ref[h, ii:ii + 1, :], (rows, tm)))

    nst = tm // sub

    def activations(st):
        cs = slice(st * sub, (st + 1) * sub)
        at = _dot_nt(u_ref[...], hb_scr[cs, :])
        act_scr[st] = _to_words(0.5 * at * (1.0 + lax.erf(at * (2.0 ** -0.5))))

    def gates(st):
        cs = slice(st * sub, (st + 1) * sub)
        for lt in range(st * (sub // LANES), (st + 1) * (sub // LANES)):
            ls = slice(lt * LANES, (lt + 1) * LANES)
            for jb0 in range(0, PEER_NKEYS // rows, _JB):
                for ii0 in range(0, ni, _IB):
                    gate = [[jnp.zeros((rows, LANES), BF16) for _ in range(_IB)] for _ in range(_JB)]
                    for h in range(PEER_HEADS):
                        js = [slice((jb0 + a) * SUBLANES, (jb0 + a + 1) * SUBLANES) for a in range(_JB)]
                        rjv = [_from_words(rj_ref[h, js[a], ls]) for a in range(_JB)]
                        e1v = [_from_words(e1_ref[h, js[a], ls]) for a in range(_JB)]
                        for b in range(_IB):
                            lm = _from_words(limb_scr[h, ii0 + b, :, ls])
                            ev = _from_words(e0b_scr[h, ii0 + b, :, ls])
                            for a in range(_JB):
                                prod = e1v[a] * ev
                                gate[a][b] = gate[a][b] + jnp.where(rjv[a] < lm, prod, jnp.zeros_like(prod))
                    for a in range(_JB):
                        for b in range(_IB):
                            w0 = (ii0 + b) * (PEER_NKEYS // pk) + (jb0 + a) * SUBLANES
                            ws = slice(w0, w0 + SUBLANES)
                            lo = slice(ls.start - cs.start, ls.stop - cs.start)
                            p_scr[st, ws, lo] = _to_words(gate[a][b] * _from_words(act_scr[st, ws, lo]))

    def accumulate(st):
        acc_scr[st] += _dot(vt_ref[...], _from_words(p_scr[st]))

    activations(0)
    for st in range(nst):
        if st + 1 < nst:
            activations(st + 1)
        if st > 0:
            accumulate(st - 1)
        gates(st)
    accumulate(nst - 1)

    @pl.when(ib == pl.num_programs(1) - 1)
    def _():
        for st in range(tm // sub):
            cs = slice(st * sub, (st + 1) * sub)
            y = acc_scr[st].T
            o_ref[cs, :] = _layer_norm(DN_ALPHA * h_ref[cs, :] + y, g_ref[...], b_ref[...])


def _peer(h1, u, vt, lim, e0, rj, e1, g, b, *, tm, ni):
    t, d = h1.shape
    n_exp = u.shape[0]
    te = ni * PEER_NKEYS
    grid = (t // tm, n_exp // te)
    pk = _pack_factor()
    words = lambda shape: pltpu.VMEM(shape, jnp.uint32)
    sub = min(tm, 2 * LANES)
    nst = tm // sub
    return pl.pallas_call(
        functools.partial(_peer_kernel, ni=ni, tm=tm, sub=sub),
        out_shape=jax.ShapeDtypeStruct((t, d), F32),
        grid=grid,
        in_specs=[pl.BlockSpec((tm, d), lambda i, j: (i, 0), pipeline_mode=pl.Buffered(1)),
                  pl.BlockSpec((te, d), lambda i, j: (j, 0)),
                  pl.BlockSpec((d, te), lambda i, j: (0, j)),
                  pl.BlockSpec((PEER_HEADS, ni, tm), lambda i, j: (0, j, i)),
                  pl.BlockSpec((PEER_HEADS, ni, tm), lambda i, j: (0, j, i)),
                  pl.BlockSpec((PEER_HEADS, PEER_NKEYS // pk, tm), lambda i, j: (0, 0, i),
                               pipeline_mode=pl.Buffered(1)),
                  pl.BlockSpec((PEER_HEADS, PEER_NKEYS // pk, tm), lambda i, j: (0, 0, i),
                               pipeline_mode=pl.Buffered(1)),
                  pl.BlockSpec(g.shape, lambda i, j: (0, 0)),
                  pl.BlockSpec(b.shape, lambda i, j: (0, 0))],
        out_specs=pl.BlockSpec((tm, d), lambda i, j: (i, 0)),
        scratch_shapes=[pltpu.VMEM((tm, d), BF16), words((nst, te // pk, sub)),
                        words((PEER_HEADS, ni, SUBLANES, tm)),
                        words((PEER_HEADS, ni, SUBLANES, tm)),
                        words((nst, te // pk, sub)), pltpu.VMEM((nst, d, sub), F32)],
        compiler_params=pltpu.CompilerParams(
            dimension_semantics=("parallel", "arbitrary"), vmem_limit_bytes=VMEM_LIMIT),
        name="peer",
    )(h1, u, vt, lim, e0, rj, e1, g, b)


def _pick_tile(n, pref):
    t = pref
    while n % t:
        t //= 2
    return t


def kernel(x, meta_tokens, emb_ln_g, emb_ln_b, w_in, w_gate_up, b_gate, b_forget, gla_norm_g,
           fox_norm_g, w_out, ln1_g, ln1_b, peer_w_q, peer_sub_keys, peer_u, peer_v, ln2_g, ln2_b):
    bsz, s, d = x.shape
    assert d == D_MODEL and s % (2 * GLA_CHUNK) == 0 and w_in.shape[0] == 1
    row = lambda v: v.reshape(1, -1).astype(F32)

    w0 = w_in[0]
    o_ga = 2 * GLA_QK + 2 * GLA_V
    o_fb = o_ga + GLA_RANK + 3 * FOX_W
    o_qb = o_ga + GLA_RANK
    o_kb, o_vb = o_qb + FOX_W, o_qb + 2 * FOX_W
    wm = jnp.concatenate([w0[:, :o_ga], w0[:, o_kb:o_vb]], axis=1).astype(BF16)
    wqv = jnp.concatenate([w0[:, o_qb:o_kb], w0[:, o_vb:o_fb]], axis=1).T.astype(BF16)
    wga =jnp.pad(w0[:, o_ga:o_ga + GLA_RANK], ((0, 0), (0, LANES - GLA_RANK))).astype(BF16)
    wft = jnp.pad(w0[:, o_fb:].T, ((0, 16 - FOX_HEADS), (0, 0))).astype(BF16)
    wup = jnp.pad(w_gate_up[0], ((0, LANES - GLA_RANK), (0, 0))).astype(BF16)
    bg = row(b_gate[0])
    bf = jnp.pad(b_forget[0].astype(F32), (0, 16 - FOX_HEADS)).reshape(16, 1)
    eg, eb = row(emb_ln_g), row(emb_ln_b)

    x_pre = jnp.concatenate([jnp.zeros((N_PAD, d), x.dtype), meta_tokens.astype(x.dtype)], axis=0)[None]
    pre = _in_proj(x_pre, eg, eb, wm, wqv, wga, wft, wup, bg, bf, tm=PREFIX)
    main = _in_proj(x, eg, eb, wm, wqv, wga, wft, wup, bg, bf, tm=_pick_tile(s, 256))
    gq_p, gk_p, gv_p, gr_p, gl_p, _, fk_p, fvt_p, lf_p = pre
    gq, gk, gv, gr, gl, fqt, fk, fvt, lf = main

    gn = row(gla_norm_g[0])
    s_zero = jnp.zeros((2, 2 * GLA_CHUNK, GLA_DV), F32)
    _, s_pre = _gla(gq_p, gk_p, gv_p, gr_p, gl_p, gn, s_zero, n_invalid=N_PAD)
    oa, _ = _gla(gq, gk, gv, gr, gl, gn, s_pre[0], n_invalid=0)

    cmeta = _cumsum_lanes(lf_p[0], rel_end=True)
    cx = _cumsum_lanes(lf.reshape(bsz * 16, s), rel_end=False).reshape(bsz, 16, s)
    ob = _fox(fqt, fk, fvt, cx, fk_p, fvt_p, cmeta, fox_norm_g[0].reshape(FOX_W, 1).astype(F32),
              tq=_pick_tile(s, 512))

    h1 = _out_proj(oa, ob, x, eg, eb, w_out[0].astype(BF16), row(ln1_g[0]), row(ln1_b[0]),
                   tm=_pick_tile(s, 512))

    t = bsz * s
    h1f = h1.reshape(t, d)
    keys = peer_sub_keys[0].reshape(2 * PEER_HEADS, PEER_NKEYS, PEER_HALF).astype(BF16)
    lim, e0, rj, e1 = _route(h1f, peer_w_q[0].astype(BF16), keys, tm=_pick_tile(t, 512))
    out = _peer(h1f, peer_u[0].astype(BF16), peer_v[0].astype(BF16).T, lim, e0, rj, e1,
                row(ln2_g[0]), row(ln2_b[0]), tm=_pick_tile(t, 1024), ni=8)
    return out.reshape(bsz, s, d)
```

```python
import functools

import jax
import jax.numpy as jnp
from jax import lax
from jax.experimental import pallas as pl
from jax.experimental.pallas import tpu as pltpu

F32 = jnp.float32
BF16 = jnp.bfloat16

D_MODEL = 1024
N_META = 16
PREFIX = 128
N_PAD = PREFIX - N_META

GLA_HEADS = 4
GLA_DK = 64
GLA_DV = 128
GLA_RANK = 16
GLA_TAU = 16.0
GLA_CHUNK = 64
GLA_QK = GLA_HEADS * GLA_DK
GLA_V = GLA_HEADS * GLA_DV

FOX_HEADS = 8
FOX_DH = 64
FOX_W = FOX_HEADS * FOX_DH

PEER_HEADS = 8
PEER_NKEYS = 128
PEER_TOPK = 16
PEER_HALF = 128

DN_ALPHA = 2.0 ** 0.25
LN_EPS = 1e-5
NEG = -1e30

LANES = 128
VMEM_LIMIT = 48 * 1024 * 1024

_PAIRS = tuple((a, b) for a in range(PEER_TOPK) for b in range(PEER_TOPK)
               if (a + 1) * (b + 1) <= PEER_TOPK)
_N_CAND = 56


def _dot(a, b):
    return jnp.dot(a, b, preferred_element_type=F32)


def _dot_nt(a, b):
    return lax.dot_general(a, b, (((1,), (1,)), ((), ())), preferred_element_type=F32)


def _layer_norm(x, g, b):
    mu = jnp.mean(x, axis=-1, keepdims=True)
    xc = x - mu
    var = jnp.mean(xc * xc, axis=-1, keepdims=True)
    return xc * lax.rsqrt(var + LN_EPS) * g + b


def _split3(x):
    hi = x.astype(BF16)
    r1 = x - hi.astype(F32)
    mid = r1.astype(BF16)
    lo = (r1 - mid.astype(F32)).astype(BF16)
    return hi, mid, lo


def _in_proj_kernel(x_ref, g_ref, b_ref, wm_ref, wqv_ref, wga_ref, wft_ref, wup_ref, bg_ref, bf_ref,
                    gq_ref, gk_ref, gv_ref, gr_ref, glog_ref, fqt_ref, fk_ref, fvt_ref, lf_ref):
    hn = _layer_norm(x_ref[0], g_ref[...], b_ref[...])
    hb = hn.astype(BF16)
    proj = _dot(hb, wm_ref[...])
    gq_ref[0] = proj[:, 0:256]
    gk_ref[0] = proj[:, 256:512]
    gv_ref[0] = proj[:, 512:1024].astype(BF16)
    gr_ref[0] = proj[:, 1024:1536]
    fk_ref[0] = proj[:, 1536:2048].astype(BF16)
    qvt = _dot_nt(wqv_ref[...], hb)
    fqt_ref[0] = (qvt[0:FOX_W, :] * (FOX_DH ** -0.5)).astype(BF16)
    fvt_ref[0] = qvt[FOX_W:2 * FOX_W, :].astype(BF16)
    ga = _dot(hb, wga_ref[...])
    z = _dot(ga.astype(BF16), wup_ref[...]) + bg_ref[...]
    glog_ref[0] = jax.nn.log_sigmoid(z) / GLA_TAU
    fbt = _dot_nt(wft_ref[...], hb)
    lf_ref[0] = jax.nn.log_sigmoid(fbt + bf_ref[...])


def _in_proj(x, ln_g, ln_b, wm, wqv, wga, wft, wup, bg, bf, *, tm):
    bsz, s, d = x.shape
    grid = (bsz, s // tm)
    tok = lambda w: pl.BlockSpec((1, tm, w), lambda b, i: (b, i, 0))
    feat = lambda r: pl.BlockSpec((1, r, tm), lambda b, i: (b, 0, i))
    full = lambda a: pl.BlockSpec(a.shape, lambda b, i: (0,) * a.ndim)
    out_shape = (
        jax.ShapeDtypeStruct((bsz, s, GLA_QK), F32),
        jax.ShapeDtypeStruct((bsz, s, GLA_QK), F32),
        jax.ShapeDtypeStruct((bsz, s, GLA_V), BF16),
        jax.ShapeDtypeStruct((bsz, s, GLA_V), F32),
        jax.ShapeDtypeStruct((bsz, s, GLA_QK), F32),
        jax.ShapeDtypeStruct((bsz, FOX_W, s), BF16),
        jax.ShapeDtypeStruct((bsz, s, FOX_W), BF16),
        jax.ShapeDtypeStruct((bsz, FOX_W, s), BF16),
        jax.ShapeDtypeStruct((bsz, 16, s), F32),
    )
    out_specs = (tok(GLA_QK), tok(GLA_QK), tok(GLA_V), tok(GLA_V), tok(GLA_QK),
                 feat(FOX_W), tok(FOX_W), feat(FOX_W), feat(16))
    return pl.pallas_call(
        _in_proj_kernel,
        out_shape=out_shape,
        grid=grid,
        in_specs=[tok(d), full(ln_g), full(ln_b), full(wm), full(wqv), full(wga), full(wft),
                  full(wup), full(bg), full(bf)],
        out_specs=out_specs,
        compiler_params=pltpu.CompilerParams(
            dimension_semantics=("parallel", "parallel"), vmem_limit_bytes=VMEM_LIMIT),
        name="in_proj",
    )(x, ln_g, ln_b, wm, wqv, wga, wft, wup, bg, bf)


def _cumsum_kernel(x_ref, o_ref, *, n_tiles, rel_end):
    ri = lax.broadcasted_iota(jnp.int32, (LANES, LANES), 0)
    ci = lax.broadcasted_iota(jnp.int32, (LANES, LANES), 1)
    upper = (ri <= ci).astype(BF16)
    carry = jnp.zeros((x_ref.shape[0], 1), F32)
    for j in range(n_tiles):
        sl = slice(j * LANES, (j + 1) * LANES)
        hi, mid, lo = _split3(x_ref[:, sl])
        cs = (_dot(hi, upper) + _dot(mid, upper)) + _dot(lo, upper) + carry
        o_ref[:, sl] = cs
        carry = cs[:, LANES - 1:LANES]
    if rel_end:
        o_ref[...] = o_ref[...] - carry


def _cumsum_lanes(x, *, rel_end):
    rows, n = x.shape
    return pl.pallas_call(
        functools.partial(_cumsum_kernel, n_tiles=n // LANES, rel_end=rel_end),
        out_shape=jax.ShapeDtypeStruct((rows, n), F32),
        name="cumsum",
    )(x)


def _gla_kernel(q_ref, k_ref, v_ref, r_ref, gl_ref, gn_ref, s0_ref, o_ref, sf_ref, s_scr,
                *, n_invalid):
    t = pl.program_id(1)

    @pl.when(t == 0)
    def _():
        s_scr[...] = s0_ref[...]

    tl = 2 * GLA_CHUNK
    ri = lax.broadcasted_iota(jnp.int32, (tl, tl), 0)
    ci = lax.broadcasted_iota(jnp.int32, (tl, tl), 1)
    same_chunk = (ri // GLA_CHUNK) == (ci // GLA_CHUNK)
    tri = ((ci <= ri) & same_chunk)
    tri_b = tri.astype(BF16)
    first = ri < GLA_CHUNK
    lane = lax.broadcasted_iota(jnp.int32, (1, tl), 1)
    row1 = lax.broadcasted_iota(jnp.int32, (tl, 1), 0)
    scale = GLA_DK ** -0.5
    valid = (row1 + t * tl) >= n_invalid

    for p in range(GLA_HEADS // 2):
        cs = slice(p * tl, (p + 1) * tl)
        g = gl_ref[0, :, cs]
        hi, mid, lo = _split3(g)
        bcum = (_dot(tri_b, hi) + _dot(tri_b, mid)) + _dot(tri_b, lo)
        half = GLA_CHUNK // 2
        bref = jnp.where(first, bcum[half - 1:half, :], bcum[GLA_CHUNK + half - 1:GLA_CHUNK + half, :])
        blast = jnp.where(first, bcum[GLA_CHUNK - 1:GLA_CHUNK, :], bcum[tl - 1:tl, :])
        q = q_ref[0, :, cs]
        k = jnp.where(valid, k_ref[0, :, cs], 0.0)
        q_in = q * jnp.exp(bcum - bref)
        k_in = (k * jnp.exp(bref - bcum)).astype(BF16)
        qe = q * jnp.exp(bcum)
        kdt = (k * jnp.exp(blast - bcum)).T
        bcum_t = bcum.T
        dec0 = jnp.exp(bcum_t[:, GLA_CHUNK - 1:GLA_CHUNK])
        dec1 = jnp.exp(bcum_t[:, tl - 1:tl])
        kdt0 = jnp.where(lane < GLA_CHUNK, kdt, 0.0).astype(BF16)
        kdt1 = jnp.where(lane >= GLA_CHUNK, kdt, 0.0).astype(BF16)

        v_h = [v_ref[0, :, (2 * p + hh) * GLA_DV:(2 * p + hh + 1) * GLA_DV] for hh in range(2)]
        s_in = s_scr[p]
        ds0 = jnp.where(first, _dot(kdt0, v_h[0]), _dot(kdt0, v_h[1]))
        ds1 = jnp.where(first, _dot(kdt1, v_h[0]), _dot(kdt1, v_h[1]))
        s_mid = dec0 * s_in + ds0
        s_out = dec1 * s_mid + ds1
        s_scr[p] = s_out
        sf_ref[0, p] = s_out
        s_in_b = s_in.astype(BF16)
        s_mid_b = s_mid.astype(BF16)

        for hh in range(2):
            h = 2 * p + hh
            mh = (lane // GLA_DK) == hh
            a = _dot_nt(jnp.where(mh, q_in, 0.0).astype(BF16), k_in) * scale
            a = jnp.where(tri, a, 0.0)
            o_intra = _dot(a.astype(BF16), v_h[hh])
            qe_m = jnp.where(mh, qe, 0.0).astype(BF16)
            o_inter = jnp.where(row1 < GLA_CHUNK, _dot(qe_m, s_in_b), _dot(qe_m, s_mid_b)) * scale
            o = o_intra + o_inter
            hs = slice(h * GLA_DV, (h + 1) * GLA_DV)
            rn = o * lax.rsqrt(jnp.mean(o * o, axis=-1, keepdims=True) + LN_EPS) * gn_ref[:, hs]
            ra = r_ref[0, :, hs]
            o_ref[0, :, hs] = (rn * (ra * jax.nn.sigmoid(ra))).astype(BF16)


def _gla(gq, gk, gv, gr, glog, gn, s0, *, n_invalid):
    bsz, s, _ = gq.shape
    tl = 2 * GLA_CHUNK
    grid = (bsz, s // tl)
    tok = lambda w: pl.BlockSpec((1, tl, w), lambda b, t: (b, t, 0))
    return pl.pallas_call(
        functools.partial(_gla_kernel, n_invalid=n_invalid),
        out_shape=(jax.ShapeDtypeStruct((bsz, s, GLA_V), BF16),
                   jax.ShapeDtypeStruct((bsz, 2, tl, GLA_DV), F32)),
        grid=grid,
        in_specs=[tok(GLA_QK), tok(GLA_QK), tok(GLA_V), tok(GLA_V), tok(GLA_QK),
                  pl.BlockSpec((1, GLA_V), lambda b, t: (0, 0)),
                  pl.BlockSpec((2, tl, GLA_DV), lambda b, t: (0, 0, 0))],
        out_specs=(tok(GLA_V), pl.BlockSpec((1, 2, tl, GLA_DV), lambda b, t: (b, 0, 0, 0))),
        scratch_shapes=[pltpu.VMEM((2, tl, GLA_DV), F32)],
        compiler_params=pltpu.CompilerParams(dimension_semantics=("parallel", "arbitrary")),
        name="gla",
    )(gq, gk, gv, gr, glog, gn, s0)


def _fox_kernel(qt_ref, k_ref, vt_ref, cx_ref, kp_ref, vpt_ref, cm_ref, g_ref, o_ref,
                cb_scr, cbm_scr, m_scr, l_scr, acc_scr, *, tq):
    hp = pl.program_id(1)
    qi = pl.program_id(2)
    n_rep = tq // LANES

    @pl.when(qi == 0)
    def _():
        rows = lax.broadcasted_iota(jnp.int32, (16, LANES), 0)
        for hh in range(2):
            pick = rows == 2 * hp + hh

            def column_tile(chunk):
                crow = jnp.sum(jnp.where(pick, chunk, 0.0), axis=0, keepdims=True)
                return jnp.broadcast_to(crow, (LANES, LANES)).T

            cbm_scr[hh] = column_tile(cm_ref[...])
            for j in range(cx_ref.shape[2] // LANES):
                sl = slice(j * LANES, (j + 1) * LANES)
                cb_scr[hh, sl, :] = column_tile(cx_ref[0, :, sl])

    qt = qt_ref[0]
    lane = lax.broadcasted_iota(jnp.int32, (1, LANES), 1)
    qstart = pl.multiple_of(qi * tq, tq)
    for hh in range(2):
        m_scr[hh] = jnp.full((1, tq), -jnp.inf, F32)
        l_scr[hh] = jnp.zeros((1, tq), F32)
        acc_scr[hh] = jnp.zeros((LANES, tq), F32)
    cref = [cb_scr[hh, pl.ds(qstart, 1), :] for hh in range(2)]

    def update(kblk, vtblk, slabs, mask):
        for hh in range(2):
            km = jnp.where((lane // FOX_DH) == hh, kblk, jnp.zeros_like(kblk))
            bias = cref[hh] - slabs[hh]
            s = _dot(km, qt) + jnp.concatenate([bias] * n_rep, axis=1)
            if mask is not None:
                s = jnp.where(mask, s, NEG)
            m_prev = m_scr[hh]
            m_new = jnp.maximum(m_prev, jnp.max(s, axis=0, keepdims=True))
            alpha = jnp.exp(m_prev - m_new)
            p = jnp.exp(s - m_new)
            l_scr[hh] = alpha * l_scr[hh] + jnp.sum(p, axis=0, keepdims=True)
            acc_scr[hh] = alpha * acc_scr[hh] + _dot(vtblk, p.astype(BF16))
            m_scr[hh] = m_new

    key_row = lax.broadcasted_iota(jnp.int32, (PREFIX, tq), 0)
    update(kp_ref[0], vpt_ref[0], [cbm_scr[0], cbm_scr[1]], key_row >= N_PAD)

    def body(j, carry):
        ks = pl.multiple_of(j * tq, tq)
        update(k_ref[0, pl.ds(ks, tq), :], vt_ref[0, :, pl.ds(ks, tq)],
               [cb_scr[0, pl.ds(ks, tq), :], cb_scr[1, pl.ds(ks, tq), :]], None)
        return carry

    lax.fori_loop(0, qi, body, 0)

    ri = lax.broadcasted_iota(jnp.int32, (tq, tq), 0)
    ci = lax.broadcasted_iota(jnp.int32, (tq, tq), 1)
    update(k_ref[0, pl.ds(qstart, tq), :], vt_ref[0, :, pl.ds(qstart, tq)],
           [cb_scr[0, pl.ds(qstart, tq), :], cb_scr[1, pl.ds(qstart, tq), :]], ri <= ci)

    first = lax.broadcasted_iota(jnp.int32, (LANES, 1), 0) < FOX_DH
    ot = jnp.where(first, acc_scr[0] / l_scr[0], acc_scr[1] / l_scr[1])
    sq = ot * ot
    ms0 = jnp.sum(jnp.where(first, sq, 0.0), axis=0, keepdims=True) / FOX_DH
    ms1 = jnp.sum(jnp.where(first, 0.0, sq), axis=0, keepdims=True) / FOX_DH
    rs = jnp.where(first, lax.rsqrt(ms0 + LN_EPS), lax.rsqrt(ms1 + LN_EPS))
    o_ref[0] = (ot * rs * g_ref[...]).T.astype(BF16)


def _fox(fqt, fk, fvt, cx, fk_pre, fvt_pre, cmeta, gcol, *, tq):
    bsz, s, _ = fk.shape
    grid = (bsz, FOX_HEADS // 2, s // tq)
    return pl.pallas_call(
        functools.partial(_fox_kernel, tq=tq),
        out_shape=jax.ShapeDtypeStruct((bsz, s, FOX_W), BF16),
        grid=grid,
        in_specs=[pl.BlockSpec((1, LANES, tq), lambda b, hp, i: (b, hp, i)),
                  pl.BlockSpec((1, s, LANES), lambda b, hp, i: (b, 0, hp)),
                  pl.BlockSpec((1, LANES, s), lambda b, hp, i: (b, hp, 0)),
                  pl.BlockSpec((1, 16, s), lambda b, hp, i: (b, 0, 0)),
                  pl.BlockSpec((1, PREFIX, LANES), lambda b, hp, i: (0, 0, hp)),
                  pl.BlockSpec((1, LANES, PREFIX), lambda b, hp, i: (0, hp, 0)),
                  pl.BlockSpec((16, PREFIX), lambda b, hp, i: (0, 0)),
                  pl.BlockSpec((LANES, 1), lambda b, hp, i: (hp, 0))],
        out_specs=pl.BlockSpec((1, tq, LANES), lambda b, hp, i: (b, i, hp)),
        scratch_shapes=[pltpu.VMEM((2, s, LANES), F32), pltpu.VMEM((2, PREFIX, LANES), F32),
                        pltpu.VMEM((2, 1, tq), F32), pltpu.VMEM((2, 1, tq), F32),
                        pltpu.VMEM((2, LANES, tq), F32)],
        compiler_params=pltpu.CompilerParams(
            dimension_semantics=("parallel", "parallel", "arbitrary")),
        name="fox",
    )(fqt, fk, fvt, cx, fk_pre, fvt_pre, cmeta, gcol)


def _out_proj_kernel(oa_ref, ob_ref, x_ref, eg_ref, eb_ref, w_ref, g_ref, b_ref, h1_ref):
    o = jnp.concatenate([oa_ref[0], ob_ref[0]], axis=-1)
    mix = _dot(o, w_ref[...])
    h0 = _layer_norm(x_ref[0], eg_ref[...], eb_ref[...])
    h1_ref[0] = _layer_norm(DN_ALPHA * h0 + mix, g_ref[...], b_ref[...])


def _out_proj(oa, ob, x, eg, eb, w, g, b, *, tm):
    bsz, s, d = x.shape
    grid = (bsz, s // tm)
    tok = lambda w_: pl.BlockSpec((1, tm, w_), lambda bb, i: (bb, i, 0))
    full = lambda a: pl.BlockSpec(a.shape, lambda bb, i: (0,) * a.ndim)
    return pl.pallas_call(
        _out_proj_kernel,
        out_shape=jax.ShapeDtypeStruct((bsz, s, d), F32),
        grid=grid,
        in_specs=[tok(GLA_V), tok(FOX_W), tok(d), full(eg), full(eb), full(w), full(g), full(b)],
        out_specs=tok(d),
        compiler_params=pltpu.CompilerParams(
            dimension_semantics=("parallel", "parallel"), vmem_limit_bytes=VMEM_LIMIT),
        name="out_proj",
    )(oa, ob, x, eg, eb, w, g, b)


def _extract_top(curs, iota, n_pick, idx_big):
    curs = list(curs)
    vals = [[] for _ in curs]
    idxs = [[] for _ in curs]
    for _ in range(n_pick):
        for c, cur in enumerate(curs):
            m = jnp.max(cur, axis=0, keepdims=True)
            idx = jnp.min(jnp.where(cur == m, iota, idx_big), axis=0, keepdims=True)
            vals[c].append(m)
            idxs[c].append(idx)
            curs[c] = jnp.where(iota == idx, -jnp.inf, cur)
    return vals, idxs, curs


_HEADS_PER_STEP = 4


def _route_kernel(h_ref, wq_ref, keys_ref, lim_ref, e0_ref, rj_ref, e1_ref,
                  q_scr, cand_scr, sel_scr, *, tm):
    q_scr[...] = _dot(h_ref[...].astype(BF16), wq_ref[...]).astype(BF16)
    cand_scr[...] = jnp.full(cand_scr.shape, -jnp.inf, F32)
    iota_n = lax.broadcasted_iota(jnp.int32, (PEER_NKEYS, LANES), 0)
    iota_c = lax.broadcasted_iota(jnp.int32, (_N_CAND, LANES), 0)
    row_start = [next(r for r, (a, _) in enumerate(_PAIRS) if a == aa) for aa in range(PEER_TOPK)]
    row_end = row_start[1:] + [len(_PAIRS)]
    hs = _HEADS_PER_STEP

    def per_tile(u, carry_u):
        ts = pl.multiple_of(u * LANES, LANES)

        def per_group(hg, carry_h):
            scores = []
            for k in range(2 * hs):
                hc = 2 * hs * hg + k
                col = pl.multiple_of(hc * PEER_HALF, PEER_HALF)
                qhc = q_scr[pl.ds(ts, LANES), pl.ds(col, PEER_HALF)]
                scores.append(_dot_nt(keys_ref[hc], qhc))
            vals, idxs, _ = _extract_top(scores, iota_n, PEER_TOPK, PEER_NKEYS)
            for j in range(hs):
                tv0, tv1 = vals[2 * j], vals[2 * j + 1]
                for r, (a, b) in enumerate(_PAIRS):
                    cand_scr[j, r:r + 1, :] = tv0[a] + tv1[b]
            cands = [cand_scr[j] for j in range(hs)]
            _, _, rest = _extract_top(cands, iota_c, PEER_TOPK, _N_CAND)
            for j in range(hs):
                h = hs * hg + j
                s0, s1 = scores[2 * j], scores[2 * j + 1]
                picked = jnp.where((rest[j] == -jnp.inf) & (iota_c < len(_PAIRS)), 1.0, 0.0)
                z = jnp.sum(picked * jnp.exp(cands[j] - cands[j][0:1, :]), axis=0, keepdims=True)
                sel_scr[j] = picked
                lim = jnp.zeros((PEER_NKEYS, LANES), F32)
                rj = jnp.full((PEER_NKEYS, LANES), float(PEER_TOPK), F32)
                for a in range(PEER_TOPK):
                    nb = jnp.sum(sel_scr[j, row_start[a]:row_end[a], :], axis=0, keepdims=True)
                    lim = jnp.where(iota_n == idxs[2 * j][a], nb, lim)
                    rj = jnp.where(iota_n == idxs[2 * j + 1][a], float(a), rj)
                lim_ref[h, :, pl.ds(ts, LANES)] = lim
                e0_ref[h, :, pl.ds(ts, LANES)] = jnp.exp(s0 - vals[2 * j][0]) * (1.0 / z)
                rj_ref[h, :, pl.ds(ts, LANES)] = _to_words(rj)
                e1_ref[h, :, pl.ds(ts, LANES)] = _to_words(jnp.exp(s1 - vals[2 * j + 1][0]))
            return carry_h

        lax.fori_loop(0, PEER_HEADS // hs, per_group, 0)
        return carry_u

    lax.fori_loop(0, tm // LANES, per_tile, 0)


def _route(h1, wq, keys, *, tm):
    t, d = h1.shape
    side = jax.ShapeDtypeStruct((PEER_HEADS, PEER_NKEYS, t), F32)
    nw = PEER_NKEYS // _pack_factor()
    side_w = jax.ShapeDtypeStruct((PEER_HEADS, nw, t), jnp.uint32)
    side_spec = pl.BlockSpec((PEER_HEADS, PEER_NKEYS, tm), lambda i: (0, 0, i))
    side_w_spec = pl.BlockSpec((PEER_HEADS, nw, tm), lambda i: (0, 0, i))
    return pl.pallas_call(
        functools.partial(_route_kernel, tm=tm),
        out_shape=(side, side, side_w, side_w),
        grid=(t // tm,),
        in_specs=[pl.BlockSpec((tm, d), lambda i: (i, 0)),
                  pl.BlockSpec(wq.shape, lambda i: (0, 0)),
                  pl.BlockSpec(keys.shape, lambda i: (0, 0, 0))],
        out_specs=(side_spec, side_spec, side_w_spec, side_w_spec),
        scratch_shapes=[pltpu.VMEM((tm, 2 * PEER_HEADS * PEER_HALF), BF16),
                        pltpu.VMEM((_HEADS_PER_STEP, _N_CAND, LANES), F32),
                        pltpu.VMEM((_HEADS_PER_STEP, _N_CAND, LANES), F32)],
        compiler_params=pltpu.CompilerParams(
            dimension_semantics=("parallel",), vmem_limit_bytes=VMEM_LIMIT),
        name="route",
    )(h1, wq, keys)


SUBLANES = 8
_JB = 2
_IB = 4


def _pack_factor():
    return 4 // jnp.dtype(BF16).itemsize


def _to_words(x):
    return pltpu.bitcast(x.astype(BF16), jnp.uint32)


def _from_words(w):
    return pltpu.bitcast(w, BF16)


def _pack_kernel(x_ref, o_ref, *, transpose):
    x = x_ref[...]
    o_ref[...] = _to_words(x.T if transpose else x)


def _pack_table(x, *, tr, transpose):
    rows, cols = x.shape
    pk = _pack_factor()
    if transpose:
        out_shape, out_block, out_map = (cols // pk, rows), (cols // pk, tr), lambda i: (0, i)
    else:
        out_shape, out_block, out_map = (rows // pk, cols), (tr // pk, cols), lambda i: (i, 0)
    return pl.pallas_call(
        functools.partial(_pack_kernel, transpose=transpose),
        out_shape=jax.ShapeDtypeStruct(out_shape, jnp.uint32),
        grid=(rows // tr,),
        in_specs=[pl.BlockSpec((tr, cols), lambda i: (i, 0))],
        out_specs=pl.BlockSpec(out_block, out_map),
        compiler_params=pltpu.CompilerParams(dimension_semantics=("parallel",)),
        name="pack_table",
    )(x)


def _peer_kernel(h_ref, u_ref, vt_ref, lim_ref, e0_ref, rj_ref, e1_ref, g_ref, b_ref, o_ref,
                 hb_scr, act_scr, limb_scr, e0b_scr, p_scr, acc_scr, *, ni, tm, sub):
    ib = pl.program_id(1)
    pk = _pack_factor()
    rows = SUBLANES * pk

    @pl.when(ib == 0)
    def _():
        hb_scr[...] = h_ref[...].astype(BF16)
        acc_scr[...] = jnp.zeros(acc_scr.shape, F32)

    for h in range(PEER_HEADS):
        for ii in range(ni):
            limb_scr[h, ii] = _to_words(jnp.broadcast_to(lim_ref[h, ii:ii + 1, :], (rows, tm)))
            e0b_scr[h, ii] = _to_words(jnp.broadcast_to(e0_ref[h, ii:ii + 1, :], (rows, tm)))

    nst = tm // sub

    def activations(st):
        cs = slice(st * sub, (st + 1) * sub)
        at = _dot_nt(_from_words(u_ref[...]), hb_scr[cs, :])
        act_scr[st] = _to_words(0.5 * at * (1.0 + lax.erf(at * (2.0 ** -0.5))))

    def gates(st):
        cs = slice(st * sub, (st + 1) * sub)
        for lt in range(st * (sub // LANES), (st + 1) * (sub // LANES)):
            ls = slice(lt * LANES, (lt + 1) * LANES)
            for jb0 in range(0, PEER_NKEYS // rows, _JB):
                for ii0 in range(0, ni, _IB):
                    gate = [[jnp.zeros((rows, LANES), BF16) for _ in range(_IB)] for _ in range(_JB)]
                    for h in range(PEER_HEADS):
                        js = [slice((jb0 + a) * SUBLANES, (jb0 + a + 1) * SUBLANES) for a in range(_JB)]
                        rjv = [_from_words(rj_ref[h, js[a], ls]) for a in range(_JB)]
                        e1v = [_from_words(e1_ref[h, js[a], ls]) for a in range(_JB)]
                        for b in range(_IB):
                            lm = _from_words(limb_scr[h, ii0 + b, :, ls])
                            ev = _from_words(e0b_scr[h, ii0 + b, :, ls])
                            for a in range(_JB):
                                prod = e1v[a] * ev
                                gate[a][b] = gate[a][b] + jnp.where(rjv[a] < lm, prod, jnp.zeros_like(prod))
                    for a in range(_JB):
                        for b in range(_IB):
                            w0 = (ii0 + b) * (PEER_NKEYS // pk) + (jb0 + a) * SUBLANES
                            ws = slice(w0, w0 + SUBLANES)
                            lo = slice(ls.start - cs.start, ls.stop - cs.start)
                            p_scr[st, ws, lo] = _to_words(gate[a][b] * _from_words(act_scr[st, ws, lo]))

    def accumulate(st):
        acc_scr[st] += _dot(_from_words(vt_ref[...]), _from_words(p_scr[st]))

    activations(0)
    for st in range(nst):
        if st + 1 < nst:
            activations(st + 1)
        if st > 0:
            accumulate(st - 1)
        gates(st)
    accumulate(nst - 1)

    @pl.when(ib == pl.num_programs(1) - 1)
    def _():
        for st in range(tm // sub):
            cs = slice(st * sub, (st + 1) * sub)
            y = acc_scr[st].T
            o_ref[cs, :] = _layer_norm(DN_ALPHA * h_ref[cs, :] + y, g_ref[...], b_ref[...])


def _peer(h1, u, vt, lim, e0, rj, e1, g, b, *, tm, ni):
    t, d = h1.shape
    pk = _pack_factor()
    n_exp = u.shape[0] * pk
    te = ni * PEER_NKEYS
    grid = (t // tm, n_exp // te)
    words = lambda shape: pltpu.VMEM(shape, jnp.uint32)
    sub = min(tm, 2 * LANES)
    nst = tm // sub
    return pl.pallas_call(
        functools.partial(_peer_kernel, ni=ni, tm=tm, sub=sub),
        out_shape=jax.ShapeDtypeStruct((t, d), F32),
        grid=grid,
        in_specs=[pl.BlockSpec((tm, d), lambda i, j: (i, 0), pipeline_mode=pl.Buffered(1)),
                  pl.BlockSpec((te // pk, d), lambda i, j: (j, 0)),
                  pl.BlockSpec((d // pk, te), lambda i, j: (0, j)),
                  pl.BlockSpec((PEER_HEADS, ni, tm), lambda i, j: (0, j, i)),
                  pl.BlockSpec((PEER_HEADS, ni, tm), lambda i, j: (0, j, i)),
                  pl.BlockSpec((PEER_HEADS, PEER_NKEYS // pk, tm), lambda i, j: (0, 0, i),
                               pipeline_mode=pl.Buffered(1)),
                  pl.BlockSpec((PEER_HEADS, PEER_NKEYS // pk, tm), lambda i, j: (0, 0, i),
                               pipeline_mode=pl.Buffered(1)),
                  pl.BlockSpec(g.shape, lambda i, j: (0, 0)),
                  pl.BlockSpec(b.shape, lambda i, j: (0, 0))],
        out_specs=pl.BlockSpec((tm, d), lambda i, j: (i, 0)),
        scratch_shapes=[pltpu.VMEM((tm, d), BF16), words((nst, te // pk, sub)),
                        words((PEER_HEADS, ni, SUBLANES, tm)),
                        words((PEER_HEADS, ni, SUBLANES, tm)),
                        words((nst, te // pk, sub)), pltpu.VMEM((nst, d, sub), F32)],
        compiler_params=pltpu.CompilerParams(
            dimension_semantics=("parallel", "arbitrary"), vmem_limit_bytes=VMEM_LIMIT),
        name="peer",
    )(h1, u, vt, lim, e0, rj, e1, g, b)


def _pick_tile(n, pref):
    t = pref
    while n % t:
        t //= 2
    return t


def kernel(x, meta_tokens, emb_ln_g, emb_ln_b, w_in, w_gate_up, b_gate, b_forget, gla_norm_g,
           fox_norm_g, w_out, ln1_g, ln1_b, peer_w_q, peer_sub_keys, peer_u, peer_v, ln2_g, ln2_b):
    bsz, s, d = x.shape
    assert d == D_MODEL and s % (2 * GLA_CHUNK) == 0 and w_in.shape[0] == 1
    row = lambda v: v.reshape(1, -1).astype(F32)

    w0 = w_in[0]
    o_ga = 2 * GLA_QK + 2 * GLA_V
    o_fb = o_ga + GLA_RANK + 3 * FOX_W
    o_qb = o_ga + GLA_RANK
    o_kb, o_vb = o_qb + FOX_W, o_qb + 2 * FOX_W
    wm = jnp.concatenate([w0[:, :o_ga], w0[:, o_kb:o_vb]], axis=1).astype(BF16)
    wqv = jnp.concatenate([w0[:, o_qb:o_kb], w0[:, o_vb:o_fb]], axis=1).T.astype(BF16)
    wga =jnp.pad(w0[:, o_ga:o_ga + GLA_RANK], ((0, 0), (0, LANES - GLA_RANK))).astype(BF16)
    wft = jnp.pad(w0[:, o_fb:].T, ((0, 16 - FOX_HEADS), (0, 0))).astype(BF16)
    wup = jnp.pad(w_gate_up[0], ((0, LANES - GLA_RANK), (0, 0))).astype(BF16)
    bg = row(b_gate[0])
    bf = jnp.pad(b_forget[0].astype(F32), (0, 16 - FOX_HEADS)).reshape(16, 1)
    eg, eb = row(emb_ln_g), row(emb_ln_b)

    x_pre = jnp.concatenate([jnp.zeros((N_PAD, d), x.dtype), meta_tokens.astype(x.dtype)], axis=0)[None]
    pre = _in_proj(x_pre, eg, eb, wm, wqv, wga, wft, wup, bg, bf, tm=PREFIX)
    main = _in_proj(x, eg, eb, wm, wqv, wga, wft, wup, bg, bf, tm=_pick_tile(s, 256))
    gq_p, gk_p, gv_p, gr_p, gl_p, _, fk_p, fvt_p, lf_p = pre
    gq, gk, gv, gr, gl, fqt, fk, fvt, lf = main

    gn = row(gla_norm_g[0])
    s_zero = jnp.zeros((2, 2 * GLA_CHUNK, GLA_DV), F32)
    _, s_pre = _gla(gq_p, gk_p, gv_p, gr_p, gl_p, gn, s_zero, n_invalid=N_PAD)
    oa, _ = _gla(gq, gk, gv, gr, gl, gn, s_pre[0], n_invalid=0)

    cmeta = _cumsum_lanes(lf_p[0], rel_end=True)
    cx = _cumsum_lanes(lf.reshape(bsz * 16, s), rel_end=False).reshape(bsz, 16, s)
    ob = _fox(fqt, fk, fvt, cx, fk_p, fvt_p, cmeta, fox_norm_g[0].reshape(FOX_W, 1).astype(F32),
              tq=_pick_tile(s, 512))

    h1 = _out_proj(oa, ob, x, eg, eb, w_out[0].astype(BF16), row(ln1_g[0]), row(ln1_b[0]),
                   tm=_pick_tile(s, 512))

    t = bsz * s
    h1f = h1.reshape(t, d)
    keys = peer_sub_keys[0].reshape(2 * PEER_HEADS, PEER_NKEYS, PEER_HALF).astype(BF16)
    lim, e0, rj, e1 = _route(h1f, peer_w_q[0].astype(BF16), keys, tm=_pick_tile(t, 512))
    u_words = _pack_table(peer_u[0], tr=512, transpose=False)
    vt_words = _pack_table(peer_v[0], tr=512, transpose=True)
    out = _peer(h1f, u_words, vt_words, lim, e0, rj, e1,
                row(ln2_g[0]), row(ln2_b[0]), tm=_pick_tile(t, 1024), ni=8)
    return out.reshape(bsz, s, d)
```
